```python
import math
import jax, jax.numpy as jnp
from jax import lax
import numpy as np

D_MODEL = 1024
BATCH = 8
SEQ = 8192
DEPTH = 1

MIX_WIDTH = D_MODEL // 2
HG_KEY_DIM = 128
HG_HEADS = MIX_WIDTH // HG_KEY_DIM
HG_VAL_DIM = MIX_WIDTH // HG_HEADS
HG_KEY_WIDTH = HG_HEADS * HG_KEY_DIM
HG_WIDTH = HG_HEADS * HG_VAL_DIM
HG_CHUNK = 64
DIFF_HEAD_DIM = 64
DIFF_HEADS = MIX_WIDTH // (2 * DIFF_HEAD_DIM)
DIFF_QK_WIDTH = DIFF_HEADS * 2 * DIFF_HEAD_DIM
DIFF_V_WIDTH = DIFF_HEADS * 2 * DIFF_HEAD_DIM
Q_BLOCK = 128
IN_WIDTH = 2 * HG_KEY_WIDTH + 2 * HG_WIDTH + 2 * DIFF_QK_WIDTH + 2 * DIFF_V_WIDTH + 2 * D_MODEL
RMS_EPS = 1e-6

kernel_name = "hybrid_hgrn2_diffattn_gated_block"


def rms_norm(x, gain):
    xf = x.astype(jnp.float32)
    xf = xf * lax.rsqrt(jnp.mean(xf * xf, axis=-1, keepdims=True) + RMS_EPS)
    return (xf * gain.astype(jnp.float32)).astype(x.dtype)


def hgrn2_chunked(q, k, v, log_f):
    B, S, H, DK = q.shape
    DV = v.shape[-1]
    N = S // HG_CHUNK

    def to_chunks(t):
        return t.reshape(B, N, HG_CHUNK, H, t.shape[-1]).transpose(1, 0, 3, 2, 4)

    qc, kc, vc, gc = to_chunks(q), to_chunks(k), to_chunks(v), to_chunks(log_f)
    bc = jnp.cumsum(gc, axis=3)
    causal = jnp.tril(jnp.ones((HG_CHUNK, HG_CHUNK), dtype=bool))[:, :, None]

    def step(state, inp):
        qt, kt, vt, bt = inp
        o_inter = jnp.einsum('bhtd,bhde->bhte', qt * jnp.exp(bt), state)
        rel = bt[:, :, :, None, :] - bt[:, :, None, :, :]
        decay = jnp.exp(jnp.where(causal, rel, -jnp.inf))
        scores = jnp.einsum('bhtd,bhtsd,bhsd->bhts', qt, decay, kt)
        o = o_inter + jnp.einsum('bhts,bhse->bhte', scores, vt)
        b_last = bt[:, :, -1:, :]
        state = (jnp.exp(b_last[:, :, 0, :])[..., None] * state
                 + jnp.einsum('bhsd,bhse->bhde', kt * jnp.exp(b_last - bt), vt))
        return state, o

    state0 = jnp.zeros((B, H, DK, DV), jnp.float32)
    _, o = lax.scan(step, state0, (qc, kc, vc, bc))
    return o.transpose(1, 0, 3, 2, 4).reshape(B, S, H, DV)


def diff_attention(q, k, v, lam):
    B, S, H, _, DH = q.shape
    NB = S // Q_BLOCK
    qb = (q.astype(jnp.float32) * (DH ** -0.5)).reshape(B, NB, Q_BLOCK, H, 2, DH).transpose(1, 0, 3, 4, 2, 5)
    kt = k.astype(jnp.float32).transpose(0, 2, 3, 1, 4)
    vt = v.astype(jnp.float32).transpose(0, 2, 1, 3)
    kpos = jnp.arange(S)

    def one_block(args):
        q_blk, blk = args
        s = jnp.einsum('bhjqd,bhjkd->bhjqk', q_blk, kt)
        qpos = blk * Q_BLOCK + jnp.arange(Q_BLOCK)
        mask = kpos[None, :] <= qpos[:, None]
        p = jax.nn.softmax(jnp.where(mask, s, -jnp.inf), axis=-1)
        a = p[:, :, 0] - lam * p[:, :, 1]
        return jnp.einsum('bhqk,bhke->bhqe', a, vt)

    o = lax.map(one_block, (qb, jnp.arange(NB)))
    return o.transpose(1, 0, 3, 2, 4).reshape(B, S, H, 2 * DH)


def setup_inputs(seed: int = 0) -> dict:
    key = jax.random.key(seed)
    ks = jax.random.split(key, 20)
    f32 = jnp.float32
    n = lambda k, shape, s: jax.random.normal(k, shape, f32) * s
    return {
        "x": n(ks[0], (BATCH, SEQ, D_MODEL), 1.0),
        "c": n(ks[1], (BATCH, D_MODEL), 1.0),
        "w_ada": n(ks[2], (DEPTH, D_MODEL, 3 * D_MODEL), D_MODEL ** -0.5),
        "b_ada": n(ks[3], (DEPTH, 3 * D_MODEL), 0.01),
        "g_pre": 1.0 + n(ks[4], (DEPTH, D_MODEL), 0.02),
        "g_post": 1.0 + n(ks[5], (DEPTH, D_MODEL), 0.02),
        "w_in": n(ks[6], (DEPTH, D_MODEL, IN_WIDTH), D_MODEL ** -0.5),
        "lb_logits": n(ks[7], (DEPTH + 1, HG_KEY_WIDTH), 0.5),
        "hg_norm_gain": 1.0 + n(ks[8], (DEPTH, HG_WIDTH), 0.02),
        "lambda_q1": n(ks[9], (DEPTH, DIFF_HEAD_DIM), 0.1),
        "lambda_k1": n(ks[10], (DEPTH, DIFF_HEAD_DIM), 0.1),
        "lambda_q2": n(ks[11], (DEPTH, DIFF_HEAD_DIM), 0.1),
        "lambda_k2": n(ks[12], (DEPTH, DIFF_HEAD_DIM), 0.1),
        "diff_norm_gain": 1.0 + n(ks[13], (DEPTH, DIFF_V_WIDTH), 0.02),
        "w_branch_a": n(ks[14], (DEPTH, HG_WIDTH, D_MODEL), HG_WIDTH ** -0.5),
        "w_branch_b": n(ks[15], (DEPTH, DIFF_V_WIDTH, D_MODEL), DIFF_V_WIDTH ** -0.5),
        "w_out": n(ks[16], (DEPTH, D_MODEL, D_MODEL), D_MODEL ** -0.5),
    }


def reference(x, c, w_ada, b_ada, g_pre, g_post, w_in, lb_logits, hg_norm_gain,
              lambda_q1, lambda_k1, lambda_q2, lambda_k2, diff_norm_gain,
              w_branch_a, w_branch_b, w_out):
    B, S, _ = x.shape
    f32 = jnp.float32
    split_sizes = [HG_KEY_WIDTH, HG_KEY_WIDTH, HG_WIDTH, HG_WIDTH,
                   DIFF_QK_WIDTH, DIFF_QK_WIDTH, DIFF_V_WIDTH, DIFF_V_WIDTH,
                   D_MODEL, D_MODEL]
    split_idx = [int(v) for v in np.cumsum(split_sizes)[:-1]]
    lb_all = jnp.cumsum(jax.nn.softmax(lb_logits.astype(f32), axis=0), axis=0)

    for l in range(DEPTH):
        mod = jax.nn.silu(c) @ w_ada[l] + b_ada[l]
        shift, scale, gate = jnp.split(mod, 3, axis=-1)
        h = rms_norm(x, g_pre[l]) * (1.0 + scale[:, None, :]) + shift[:, None, :]

        proj = h @ w_in[l]
        hq, hf, hi, hg, dq, dk, dv, dg, ma, mb = jnp.split(proj, split_idx, axis=-1)

        lb = lb_all[l].reshape(HG_HEADS, HG_KEY_DIM)
        log_lb, log_1m_lb = jnp.log(lb), jnp.log1p(-lb)
        fz = hf.reshape(B, S, HG_HEADS, HG_KEY_DIM).astype(f32)
        log_f = jnp.logaddexp(log_lb, log_1m_lb + jax.nn.log_sigmoid(fz))
        k_in = jnp.exp(log_1m_lb + jax.nn.log_sigmoid(-fz))
        o_a = hgrn2_chunked(hq.reshape(B, S, HG_HEADS, HG_KEY_DIM).astype(f32), k_in,
                            hi.reshape(B, S, HG_HEADS, HG_VAL_DIM).astype(f32), log_f)
        o_a = rms_norm(o_a, hg_norm_gain[l].reshape(HG_HEADS, HG_VAL_DIM))
        o_a = o_a.reshape(B, S, HG_WIDTH).astype(x.dtype) * jax.nn.silu(hg)

        lam_init = 0.8 - 0.6 * math.exp(-0.3 * l)
        lam = (jnp.exp(jnp.sum(lambda_q1[l].astype(f32) * lambda_k1[l].astype(f32)))
               - jnp.exp(jnp.sum(lambda_q2[l].astype(f32) * lambda_k2[l].astype(f32))) + lam_init)
        o_b = diff_attention(dq.reshape(B, S, DIFF_HEADS, 2, DIFF_HEAD_DIM),
                             dk.reshape(B, S, DIFF_HEADS, 2, DIFF_HEAD_DIM),
                             dv.reshape(B, S, DIFF_HEADS, 2 * DIFF_HEAD_DIM), lam)
        o_b = rms_norm(o_b, diff_norm_gain[l].reshape(DIFF_HEADS, 2 * DIFF_HEAD_DIM)) * (1.0 - lam_init)
        o_b = o_b.reshape(B, S, DIFF_V_WIDTH).astype(x.dtype) * jax.nn.silu(dg)

        y = jax.nn.sigmoid(ma) * (o_a @ w_branch_a[l]) + jax.nn.sigmoid(mb) * (o_b @ w_branch_b[l])
        y = y @ w_out[l]

        x = x + gate[:, None, :] * rms_norm(y, g_post[l])
    return x
```

```python
import functools
import math

import numpy as np
import jax
import jax.numpy as jnp
from jax import lax
from jax.experimental import pallas as pl
from jax.experimental.pallas import tpu as pltpu

F32 = jnp.float32
BF16 = jnp.bfloat16
RMS_EPS = 1e-6
LOG2E = 1.4426950408889634

HEAD = 128
DIFF_HEAD_DIM = 64
N_HEADS = 4
MIX = N_HEADS * HEAD
CHUNK = 128
SUB = 8
LEVELS = (8, 16, 32, 64)
LAM_INIT = 0.8 - 0.6 * math.exp(-0.3 * 0)
ONES_ROWS = 16

V7X_VMEM_BYTES = 64 * 1024 * 1024
NT_DIMS = (((1,), (1,)), ((), ()))


def _sigmoid(z):
    return jax.nn.sigmoid(z)


def _prep_kernel(c_ref, w_ref, b_ref, lbl_ref, lq1_ref, lk1_ref, lq2_ref, lk2_ref,
                 mod_ref, lb_ref, lam_ref):
    c = c_ref[...]
    sc = c * _sigmoid(c)
    mod_ref[...] = jnp.dot(sc, w_ref[...], precision=lax.Precision.HIGHEST,
                           preferred_element_type=F32) + b_ref[...]
    z = lbl_ref[...]
    e = jnp.exp(z - jnp.max(z, axis=0, keepdims=True))
    lb_ref[...] = e / jnp.sum(e, axis=0, keepdims=True)
    d1 = jnp.sum(lq1_ref[...] * lk1_ref[...], axis=-1, keepdims=True)
    d2 = jnp.sum(lq2_ref[...] * lk2_ref[...], axis=-1, keepdims=True)
    lam = jnp.exp(d1) - jnp.exp(d2) + LAM_INIT
    lam_ref[...] = jnp.broadcast_to(lam, lam_ref.shape)


def _prep(c, w_ada, b_ada, lb_logits, lq1, lk1, lq2, lk2):
    B, D = c.shape
    n_tiles = w_ada.shape[1] // D
    const = lambda j: (0, 0)
    return pl.pallas_call(
        _prep_kernel,
        grid=(n_tiles,),
        in_specs=[
            pl.BlockSpec((B, D), const),
            pl.BlockSpec((D, D), lambda j: (0, j)),
            pl.BlockSpec((1, D), lambda j: (0, j)),
            pl.BlockSpec(lb_logits.shape, const),
            pl.BlockSpec(lq1.shape, const),
            pl.BlockSpec(lk1.shape, const),
            pl.BlockSpec(lq2.shape, const),
            pl.BlockSpec(lk2.shape, const),
        ],
        out_specs=[
            pl.BlockSpec((B, D), lambda j: (0, j)),
            pl.BlockSpec(lb_logits.shape, const),
            pl.BlockSpec((1, HEAD), const),
        ],
        out_shape=[
            jax.ShapeDtypeStruct((B, n_tiles * D), F32),
            jax.ShapeDtypeStruct(lb_logits.shape, F32),
            jax.ShapeDtypeStruct((1, HEAD), F32),
        ],
        compiler_params=pltpu.CompilerParams(dimension_semantics=("arbitrary",)),
        name="prep",
    )(c, w_ada, b_ada, lb_logits, lq1, lk1, lq2, lk2)


def _inproj_kernel(x_ref, mod_ref, gpre_ref, lb_ref, w_ref,
                   hq_ref, kin_ref, logf_ref, hi_ref, sg_ref,
                   dq_ref, dk_ref, dvt_ref, sdg_ref, gma_ref, gmb_ref):
    x = x_ref[...]
    ms = jnp.mean(x * x, axis=-1, keepdims=True)
    xn = x * lax.rsqrt(ms + RMS_EPS) * gpre_ref[...]
    h = xn * (1.0 + mod_ref[1:2, :]) + mod_ref[0:1, :]
    hb = h.astype(BF16)

    def proj(c0, width):
        return jnp.dot(hb, w_ref[:, c0:c0 + width], preferred_element_type=F32)

    hq_ref[...] = proj(0 * MIX, MIX).astype(BF16)
    sig = _sigmoid(proj(1 * MIX, MIX))
    lb = lb_ref[0:1, :]
    oml = lb_ref[1:2, :]
    logf_ref[...] = jnp.log(lb + oml * sig)
    kin_ref[...] = (oml * (1.0 - sig)).astype(BF16)
    hi_ref[...] = proj(2 * MIX, MIX).astype(BF16)
    z = proj(3 * MIX, MIX)
    sg_ref[...] = (z * _sigmoid(z)).astype(BF16)
    dq_ref[...] = (proj(4 * MIX, MIX) * (DIFF_HEAD_DIM ** -0.5 * LOG2E)).astype(BF16)
    dk_ref[...] = proj(5 * MIX, MIX).astype(BF16)
    dvt_ref[...] = proj(6 * MIX, MIX).T.astype(BF16)
    z = proj(7 * MIX, MIX)
    sdg_ref[...] = (z * _sigmoid(z)).astype(BF16)
    d_model = gma_ref.shape[-1]
    gma_ref[...] = _sigmoid(proj(8 * MIX, d_model)).astype(BF16)
    gmb_ref[...] = _sigmoid(proj(8 * MIX + d_model, d_model)).astype(BF16)


def _inproj(x, mod3, g_pre, lbv, w_in_bf, tm):
    B, S, D = x.shape
    n_in = w_in_bf.shape[1]
    nt = S // tm
    tok = lambda width: pl.BlockSpec((None, tm, width), lambda b, i: (b, i, 0))
    const2 = lambda b, i: (0, 0)
    sds = lambda width, dt: jax.ShapeDtypeStruct((B, S, width), dt)
    return pl.pallas_call(
        _inproj_kernel,
        grid=(B, nt),
        in_specs=[
            tok(D),
            pl.BlockSpec((None, 3, D), lambda b, i: (b, 0, 0)),
            pl.BlockSpec((1, D), const2),
            pl.BlockSpec(lbv.shape, const2),
            pl.BlockSpec((D, n_in), const2, pipeline_mode=pl.Buffered(1)),
        ],
        out_specs=[
            tok(MIX), tok(MIX), tok(MIX), tok(MIX), tok(MIX), tok(MIX), tok(MIX),
            pl.BlockSpec((None, None, MIX, tm), lambda b, i: (b, i, 0, 0)),
            tok(MIX), tok(D), tok(D),
        ],
        out_shape=[
            sds(MIX, BF16), sds(MIX, BF16), sds(MIX, F32), sds(MIX, BF16), sds(MIX, BF16),
            sds(MIX, BF16), sds(MIX, BF16),
            jax.ShapeDtypeStruct((B, nt, MIX, tm), BF16),
            sds(MIX, BF16), sds(D, BF16), sds(D, BF16),
        ],
        compiler_params=pltpu.CompilerParams(
            dimension_semantics=("parallel", "parallel"),
            vmem_limit_bytes=V7X_VMEM_BYTES * 7 // 8),
        name="inproj",
    )(x, mod3, g_pre, lbv, w_in_bf)


def _level_codes():
    t = np.arange(CHUNK)[:, None]
    s = np.arange(CHUNK)[None, :]
    code = np.zeros((CHUNK, CHUNK), np.int32)
    for i, m in reversed(list(enumerate(LEVELS))):
        code = np.where((t // (2 * m) == s // (2 * m)) & (s < t), 1 + i, code)
    code = np.where((t // SUB == s // SUB) & (s <= t), 1 + len(LEVELS) + s % SUB, code)
    return code.astype(np.int32)


def _split3(x):
    h1 = x.astype(BF16)
    r1 = x - h1.astype(F32)
    h2 = r1.astype(BF16)
    h3 = (r1 - h2.astype(F32)).astype(BF16)
    return h1, h2, h3


def _block_row(a, block, row):
    g = CHUNK // block
    a3 = a.reshape(g, block, HEAD)
    return jnp.broadcast_to(a3[:, row:row + 1, :], (g, block, HEAD)).reshape(CHUNK, HEAD)


def _hgrn2_kernel(hq_ref, kin_ref, logf_ref, hi_ref, sg_ref, gain_ref, tril_ref, lvl_ref,
                  out_ref, state_ref):
    @pl.when(pl.program_id(1) == 0)
    def _():
        state_ref[...] = jnp.zeros_like(state_ref)

    tril = tril_ref[...]
    lvl = lvl_ref[...]
    ones = jnp.ones((HEAD, HEAD), BF16)
    n_chunks = hq_ref.shape[0] // CHUNK
    for ci in range(n_chunks):
        rows = slice(ci * CHUNK, (ci + 1) * CHUNK)
        l1, l2, l3 = _split3(logf_ref[rows, :])
        b_all = (jnp.dot(tril, l3, preferred_element_type=F32)
                 + jnp.dot(tril, l2, preferred_element_type=F32)
                 + jnp.dot(tril, l1, preferred_element_type=F32))
        for h in range(N_HEADS):
            cols = slice(h * HEAD, (h + 1) * HEAD)
            b = b_all[:, cols]
            q = hq_ref[rows, cols].astype(F32)
            k = kin_ref[rows, cols].astype(F32)
            vb = hi_ref[rows, cols]
            b_last = b[CHUNK - 1:CHUNK, :]

            state_t = state_ref[h]
            qbar = (q * jnp.exp(b)).astype(BF16)
            o = lax.dot_general(qbar, state_t.astype(BF16), NT_DIMS, preferred_element_type=F32)
            kbar = (k * jnp.exp(b_last - b)).astype(BF16)
            v_t = vb.astype(F32).T.astype(BF16)
            state_ref[h] = state_t * jnp.exp(b_last) + jnp.dot(v_t, kbar, preferred_element_type=F32)

            scores = jnp.zeros((CHUNK, CHUNK), F32)
            for i, m in enumerate(LEVELS):
                piv = _block_row(b, 2 * m, m - 1)
                qm = (q * jnp.exp(jnp.minimum(b - piv, 0.0))).astype(BF16)
                km = (k * jnp.exp(jnp.minimum(piv - b, 0.0))).astype(BF16)
                am = lax.dot_general(qm, km, NT_DIMS, preferred_element_type=F32)
                scores = jnp.where(lvl == 1 + i, am, scores)
            for j in range(SUB):
                kj = _block_row(k, SUB, j)
                bj = _block_row(b, SUB, j)
                pj = (q * kj * jnp.exp(jnp.minimum(b - bj, 0.0))).astype(BF16)
                rj = jnp.dot(pj, ones, preferred_element_type=F32)
                scores = jnp.where(lvl == 1 + len(LEVELS) + j, rj, scores)
            o = o + jnp.dot(scores.astype(BF16), vb, preferred_element_type=F32)

            o = o * lax.rsqrt(jnp.mean(o * o, axis=-1, keepdims=True) + RMS_EPS) * gain_ref[:, cols]
            out_ref[rows, cols] = (o * sg_ref[rows, cols].astype(F32)).astype(BF16)


def _hgrn2(hq, kin, logf, hi, sg, gain, tc):
    B, S, _ = hq.shape
    tril = jnp.asarray(np.tril(np.ones((CHUNK, CHUNK), np.float32)), BF16)
    lvl = jnp.asarray(_level_codes())
    tok = pl.BlockSpec((None, tc, MIX), lambda b, i: (b, i, 0))
    const2 = lambda b, i: (0, 0)
    return pl.pallas_call(
        _hgrn2_kernel,
        grid=(B, S // tc),
        in_specs=[tok, tok, tok, tok, tok,
                  pl.BlockSpec((1, MIX), const2),
                  pl.BlockSpec((CHUNK, CHUNK), const2),
                  pl.BlockSpec((CHUNK, CHUNK), const2)],
        out_specs=tok,
        out_shape=jax.ShapeDtypeStruct((B, S, MIX), BF16),
        scratch_shapes=[pltpu.VMEM((N_HEADS, HEAD, HEAD), F32)],
        compiler_params=pltpu.CompilerParams(dimension_semantics=("arbitrary", "arbitrary")),
        name="hgrn2",
    )(hq, kin, logf, hi, sg, gain, tril, lvl)


def _attn_kernel(lam_ref, q_ref, k_ref, vt_ref, sdg_ref, gain_ref, o_ref, m_ref, acc_ref):
    tq = q_ref.shape[0]
    tk = vt_ref.shape[-1]
    i = pl.program_id(2)
    q = q_ref[...]
    lane = lax.broadcasted_iota(jnp.int32, q.shape, 1)
    zero = jnp.zeros_like(q)
    qz = jnp.concatenate([jnp.where(lane < DIFF_HEAD_DIM, q, zero),
                          jnp.where(lane >= DIFF_HEAD_DIM, q, zero)], axis=0)
    ones_rows = jnp.ones((ONES_ROWS, tk), BF16)

    m_ref[...] = jnp.full(m_ref.shape, -jnp.inf, F32)
    acc_ref[...] = jnp.zeros(acc_ref.shape, F32)

    def block(j, masked):
        kb = k_ref[pl.ds(pl.multiple_of(j * tk, tk), tk), :]
        s = lax.dot_general(kb, qz, NT_DIMS, preferred_element_type=F32)
        if masked:
            key = lax.broadcasted_iota(jnp.int32, s.shape, 0)
            col = lax.broadcasted_iota(jnp.int32, s.shape, 1)
            qry = jnp.where(col >= tq, col - tq, col)
            s = jnp.where(key <= qry, s, -jnp.inf)
        m_old = m_ref[...]
        m_new = jnp.maximum(m_old, jnp.max(s, axis=0, keepdims=True))
        alpha = jnp.exp2(m_old - m_new)
        p = jnp.exp2(s - m_new).astype(BF16)
        vt = jnp.concatenate([vt_ref[j], ones_rows], axis=0)
        acc_ref[...] = alpha * acc_ref[...] + jnp.dot(vt, p, preferred_element_type=F32)
        m_ref[...] = m_new

    def body(j, carry):
        block(j, False)
        return carry

    lax.fori_loop(0, i, body, 0)
    block(i, True)

    acc = acc_ref[...]
    o_t = acc[0:HEAD, :] * (1.0 / acc[HEAD:HEAD + 1, :])
    o = (o_t[:, :tq] - lam_ref[0] * o_t[:, tq:]).T
    o = o * lax.rsqrt(jnp.mean(o * o, axis=-1, keepdims=True) + RMS_EPS) * gain_ref[...]
    o_ref[...] = (o * (1.0 - LAM_INIT) * sdg_ref[...].astype(F32)).astype(BF16)


def _attn(lam, dq, dk, dvt, sdg, gain, tq):
    B, S, _ = dq.shape
    nkb, tk = dvt.shape[1], dvt.shape[3]
    assert tq == tk, "the diagonal block mask assumes equal query and key block lengths"
    qblk = pl.BlockSpec((None, tq, HEAD), lambda b, h, i: (b, i, h))
    return pl.pallas_call(
        _attn_kernel,
        grid=(B, N_HEADS, S // tq),
        in_specs=[
            pl.BlockSpec(memory_space=pltpu.SMEM),
            qblk,
            pl.BlockSpec((None, S, HEAD), lambda b, h, i: (b, 0, h)),
            pl.BlockSpec((None, nkb, HEAD, tk), lambda b, h, i: (b, 0, h, 0)),
            qblk,
            pl.BlockSpec((1, HEAD), lambda b, h, i: (0, h)),
        ],
        out_specs=qblk,
        out_shape=jax.ShapeDtypeStruct((B, S, MIX), BF16),
        scratch_shapes=[pltpu.VMEM((1, 2 * tq), F32),
                        pltpu.VMEM((HEAD + ONES_ROWS, 2 * tq), F32)],
        compiler_params=pltpu.CompilerParams(
            dimension_semantics=("parallel", "parallel", "arbitrary"),
            vmem_limit_bytes=V7X_VMEM_BYTES * 3 // 4),
        name="attn",
    )(lam, dq, dk, dvt, sdg, gain)


def _outproj_kernel(oa_ref, ob_ref, gma_ref, gmb_ref, x_ref, mod_ref, gpost_ref,
                    wa_ref, wb_ref, wo_ref, out_ref):
    ya = jnp.dot(oa_ref[...], wa_ref[...], preferred_element_type=F32)
    yb = jnp.dot(ob_ref[...], wb_ref[...], preferred_element_type=F32)
    y = gma_ref[...].astype(F32) * ya + gmb_ref[...].astype(F32) * yb
    y = jnp.dot(y.astype(BF16), wo_ref[...], preferred_element_type=F32)
    yn = y * lax.rsqrt(jnp.mean(y * y, axis=-1, keepdims=True) + RMS_EPS) * gpost_ref[...]
    out_ref[...] = x_ref[...] + mod_ref[2:3, :] * yn


def _outproj(oa, ob, gma, gmb, x, mod3, g_post, wa, wb, wo, tm):
    B, S, D = x.shape
    tok = lambda width: pl.BlockSpec((None, tm, width), lambda b, i: (b, i, 0))
    const2 = lambda b, i: (0, 0)
    return pl.pallas_call(
        _outproj_kernel,
        grid=(B, S // tm),
        in_specs=[
            tok(MIX), tok(MIX), tok(D), tok(D), tok(D),
            pl.BlockSpec((None, 3, D), lambda b, i: (b, 0, 0)),
            pl.BlockSpec((1, D), const2),
            pl.BlockSpec((MIX, D), const2),
            pl.BlockSpec((MIX, D), const2),
            pl.BlockSpec((D, D), const2),
        ],
        out_specs=tok(D),
        out_shape=jax.ShapeDtypeStruct((B, S, D), x.dtype),
        compiler_params=pltpu.CompilerParams(
            dimension_semantics=("parallel", "parallel"),
            vmem_limit_bytes=V7X_VMEM_BYTES * 3 // 4),
        name="outproj",
    )(oa, ob, gma, gmb, x, mod3, g_post, wa, wb, wo)


def kernel(x, c, w_ada, b_ada, g_pre, g_post, w_in, lb_logits, hg_norm_gain,
           lambda_q1, lambda_k1, lambda_q2, lambda_k2, diff_norm_gain,
           w_branch_a, w_branch_b, w_out):
    B, S, D = x.shape
    assert w_in.shape[0] == 1 and lb_logits.shape[0] == 2, "single-layer block"
    assert w_in.shape[2] == 8 * MIX + 2 * D
    tm = min(512, S)
    tc = min(256, S)
    assert S % tm == 0 and S % tc == 0 and tc % CHUNK == 0

    mod, lbv, lam = _prep(c, w_ada[0], b_ada, lb_logits, lambda_q1, lambda_k1, lambda_q2, lambda_k2)
    mod3 = mod.reshape(B, 3, D)

    hq, kin, logf, hi, sg, dq, dk, dvt, sdg, gma, gmb = _inproj(
        x, mod3, g_pre, lbv, w_in[0].astype(BF16), tm)

    o_a = _hgrn2(hq, kin, logf, hi, sg, hg_norm_gain, tc)
    o_b = _attn(lam[0, :1], dq, dk, dvt, sdg, diff_norm_gain, tm)

    return _outproj(o_a, o_b, gma, gmb, x, mod3, g_post,
                    w_branch_a[0].astype(BF16), w_branch_b[0].astype(BF16), w_out[0].astype(BF16), tm)
```

```python
import functools
import math

import numpy as np
import jax
import jax.numpy as jnp
from jax import lax
from jax.experimental import pallas as pl
from jax.experimental.pallas import tpu as pltpu

F32 = jnp.float32
BF16 = jnp.bfloat16
RMS_EPS = 1e-6
LOG2E = 1.4426950408889634

HEAD = 128
DIFF_HEAD_DIM = 64
N_HEADS = 4
MIX = N_HEADS * HEAD
CHUNK = 128
SUB = 8
LEVELS = (8, 16, 32, 64)
LAM_INIT = 0.8 - 0.6 * math.exp(-0.3 * 0)
ONES_ROWS = 16

V7X_VMEM_BYTES = 64 * 1024 * 1024
NT_DIMS = (((1,), (1,)), ((), ()))


def _sigmoid(z):
    return jax.nn.sigmoid(z)


def _prep_kernel(c_ref, w_ref, b_ref, lbl_ref, lq1_ref, lk1_ref, lq2_ref, lk2_ref,
                 mod_ref, lb_ref, lam_ref):
    c = c_ref[...]
    sc = c * _sigmoid(c)
    mod_ref[...] = jnp.dot(sc, w_ref[...], precision=lax.Precision.HIGHEST,
                           preferred_element_type=F32) + b_ref[...]
    z = lbl_ref[...]
    e = jnp.exp(z - jnp.max(z, axis=0, keepdims=True))
    lb_ref[...] = e / jnp.sum(e, axis=0, keepdims=True)
    d1 = jnp.sum(lq1_ref[...] * lk1_ref[...], axis=-1, keepdims=True)
    d2 = jnp.sum(lq2_ref[...] * lk2_ref[...], axis=-1, keepdims=True)
    lam = jnp.exp(d1) - jnp.exp(d2) + LAM_INIT
    lam_ref[...] = jnp.broadcast_to(lam, lam_ref.shape)


def _prep(c, w_ada, b_ada, lb_logits, lq1, lk1, lq2, lk2):
    B, D = c.shape
    n_tiles = w_ada.shape[1] // D
    const = lambda j: (0, 0)
    return pl.pallas_call(
        _prep_kernel,
        grid=(n_tiles,),
        in_specs=[
            pl.BlockSpec((B, D), const),
            pl.BlockSpec((D, D), lambda j: (0, j)),
            pl.BlockSpec((1, D), lambda j: (0, j)),
            pl.BlockSpec(lb_logits.shape, const),
            pl.BlockSpec(lq1.shape, const),
            pl.BlockSpec(lk1.shape, const),
            pl.BlockSpec(lq2.shape, const),
            pl.BlockSpec(lk2.shape, const),
        ],
        out_specs=[
            pl.BlockSpec((B, D), lambda j: (0, j)),
            pl.BlockSpec(lb_logits.shape, const),
            pl.BlockSpec((1, HEAD), const),
        ],
        out_shape=[
            jax.ShapeDtypeStruct((B, n_tiles * D), F32),
            jax.ShapeDtypeStruct(lb_logits.shape, F32),
            jax.ShapeDtypeStruct((1, HEAD), F32),
        ],
        compiler_params=pltpu.CompilerParams(dimension_semantics=("arbitrary",)),
        name="prep",
    )(c, w_ada, b_ada, lb_logits, lq1, lk1, lq2, lk2)


def _inproj_kernel(x_ref, mod_ref, gpre_ref, lb_ref, w_ref,
                   hq_ref, kin_ref, logf_ref, hi_ref, sg_ref,
                   dq_ref, dk_ref, dvt_ref, sdg_ref, gma_ref, gmb_ref):
    x = x_ref[...]
    ms = jnp.mean(x * x, axis=-1, keepdims=True)
    xn = x * lax.rsqrt(ms + RMS_EPS) * gpre_ref[...]
    h = xn * (1.0 + mod_ref[1:2, :]) + mod_ref[0:1, :]
    hb = h.astype(BF16)

    def proj(c0, width):
        return jnp.dot(hb, w_ref[:, c0:c0 + width], preferred_element_type=F32)

    hq_ref[...] = proj(0 * MIX, MIX).astype(BF16)
    sig = _sigmoid(proj(1 * MIX, MIX))
    lb = lb_ref[0:1, :]
    oml = lb_ref[1:2, :]
    logf_ref[...] = jnp.log(lb + oml * sig)
    kin_ref[...] = (oml * (1.0 - sig)).astype(BF16)
    hi_ref[...] = proj(2 * MIX, MIX).astype(BF16)
    z = proj(3 * MIX, MIX)
    sg_ref[...] = (z * _sigmoid(z)).astype(BF16)
    dq_ref[...] = (proj(4 * MIX, MIX) * (DIFF_HEAD_DIM ** -0.5 * LOG2E)).astype(BF16)
    dk_ref[...] = proj(5 * MIX, MIX).astype(BF16)
    dv_t = proj(6 * MIX, MIX).T.astype(BF16)
    tk = dvt_ref.shape[-1]
    for u in range(dvt_ref.shape[0]):
        dvt_ref[u] = dv_t[:, u * tk:(u + 1) * tk]
    z = proj(7 * MIX, MIX)
    sdg_ref[...] = (z * _sigmoid(z)).astype(BF16)
    d_model = gma_ref.shape[-1]
    gma_ref[...] = _sigmoid(proj(8 * MIX, d_model)).astype(BF16)
    gmb_ref[...] = _sigmoid(proj(8 * MIX + d_model, d_model)).astype(BF16)


def _inproj(x, mod3, g_pre, lbv, w_in_bf, tm, tk):
    B, S, D = x.shape
    n_in = w_in_bf.shape[1]
    nt = S // tm
    ku = tm // tk
    tok = lambda width: pl.BlockSpec((None, tm, width), lambda b, i: (b, i, 0))
    const2 = lambda b, i: (0, 0)
    sds = lambda width, dt: jax.ShapeDtypeStruct((B, S, width), dt)
    return pl.pallas_call(
        _inproj_kernel,
        grid=(B, nt),
        in_specs=[
            tok(D),
            pl.BlockSpec((None, 3, D), lambda b, i: (b, 0, 0)),
            pl.BlockSpec((1, D), const2),
            pl.BlockSpec(lbv.shape, const2),
            pl.BlockSpec((D, n_in), const2, pipeline_mode=pl.Buffered(1)),
        ],
        out_specs=[
            tok(MIX), tok(MIX), tok(MIX), tok(MIX), tok(MIX), tok(MIX), tok(MIX),
            pl.BlockSpec((None, ku, MIX, tk), lambda b, i: (b, i, 0, 0)),
            tok(MIX), tok(D), tok(D),
        ],
        out_shape=[
            sds(MIX, BF16), sds(MIX, BF16), sds(MIX, F32), sds(MIX, BF16), sds(MIX, BF16),
            sds(MIX, BF16), sds(MIX, BF16),
            jax.ShapeDtypeStruct((B, nt * ku, MIX, tk), BF16),
            sds(MIX, BF16), sds(D, BF16), sds(D, BF16),
        ],
        compiler_params=pltpu.CompilerParams(
            dimension_semantics=("parallel", "parallel"),
            vmem_limit_bytes=V7X_VMEM_BYTES * 7 // 8),
        name="inproj",
    )(x, mod3, g_pre, lbv, w_in_bf)


def _level_codes():
    t = np.arange(CHUNK)[:, None]
    s = np.arange(CHUNK)[None, :]
    code = np.zeros((CHUNK, CHUNK), np.int32)
    for i, m in reversed(list(enumerate(LEVELS))):
        code = np.where((t // (2 * m) == s // (2 * m)) & (s < t), 1 + i, code)
    code = np.where((t // SUB == s // SUB) & (s <= t), 1 + len(LEVELS) + s % SUB, code)
    return code.astype(np.int32)


def _split3(x):
    h1 = x.astype(BF16)
    r1 = x - h1.astype(F32)
    h2 = r1.astype(BF16)
    h3 = (r1 - h2.astype(F32)).astype(BF16)
    return h1, h2, h3


def _block_row(a, block, row):
    g = CHUNK // block
    a3 = a.reshape(g, block, HEAD)
    return jnp.broadcast_to(a3[:, row:row + 1, :], (g, block, HEAD)).reshape(CHUNK, HEAD)


def _hgrn2_kernel(hq_ref, kin_ref, logf_ref, hi_ref, sg_ref, gain_ref, tril_ref, lvl_ref,
                  out_ref, state_ref):
    @pl.when(pl.program_id(1) == 0)
    def _():
        state_ref[...] = jnp.zeros_like(state_ref)

    tril = tril_ref[...]
    lvl = lvl_ref[...]
    ones = jnp.ones((HEAD, HEAD), BF16)
    n_chunks = hq_ref.shape[0] // CHUNK
    for ci in range(n_chunks):
        rows = slice(ci * CHUNK, (ci + 1) * CHUNK)
        l1, l2, l3 = _split3(logf_ref[rows, :])
        b_all = (jnp.dot(tril, l3, preferred_element_type=F32)
                 + jnp.dot(tril, l2, preferred_element_type=F32)
                 + jnp.dot(tril, l1, preferred_element_type=F32))
        for h in range(N_HEADS):
            cols = slice(h * HEAD, (h + 1) * HEAD)
            b = b_all[:, cols]
            q = hq_ref[rows, cols].astype(F32)
            k = kin_ref[rows, cols].astype(F32)
            vb = hi_ref[rows, cols]
            b_last = b[CHUNK - 1:CHUNK, :]

            state_t = state_ref[h]
            qbar = (q * jnp.exp(b)).astype(BF16)
            o = lax.dot_general(qbar, state_t.astype(BF16), NT_DIMS, preferred_element_type=F32)
            kbar = (k * jnp.exp(b_last - b)).astype(BF16)
            v_t = vb.astype(F32).T.astype(BF16)
            state_ref[h] = state_t * jnp.exp(b_last) + jnp.dot(v_t, kbar, preferred_element_type=F32)

            scores = jnp.zeros((CHUNK, CHUNK), F32)
            for i, m in enumerate(LEVELS):
                piv = _block_row(b, 2 * m, m - 1)
                qm = (q * jnp.exp(jnp.minimum(b - piv, 0.0))).astype(BF16)
                km = (k * jnp.exp(jnp.minimum(piv - b, 0.0))).astype(BF16)
                am = lax.dot_general(qm, km, NT_DIMS, preferred_element_type=F32)
                scores = jnp.where(lvl == 1 + i, am, scores)
            for j in range(SUB):
                kj = _block_row(k, SUB, j)
                bj = _block_row(b, SUB, j)
                pj = (q * kj * jnp.exp(jnp.minimum(b - bj, 0.0))).astype(BF16)
                rj = jnp.dot(pj, ones, preferred_element_type=F32)
                scores = jnp.where(lvl == 1 + len(LEVELS) + j, rj, scores)
            o = o + jnp.dot(scores.astype(BF16), vb, preferred_element_type=F32)

            o = o * lax.rsqrt(jnp.mean(o * o, axis=-1, keepdims=True) + RMS_EPS) * gain_ref[:, cols]
            out_ref[rows, cols] = (o * sg_ref[rows, cols].astype(F32)).astype(BF16)


def _hgrn2(hq, kin, logf, hi, sg, gain, tc):
    B, S, _ = hq.shape
    tril = jnp.asarray(np.tril(np.ones((CHUNK, CHUNK), np.float32)), BF16)
    lvl = jnp.asarray(_level_codes())
    tok = pl.BlockSpec((None, tc, MIX), lambda b, i: (b, i, 0))
    const2 = lambda b, i: (0, 0)
    return pl.pallas_call(
        _hgrn2_kernel,
        grid=(B, S // tc),
        in_specs=[tok, tok, tok, tok, tok,
                  pl.BlockSpec((1, MIX), const2),
                  pl.BlockSpec((CHUNK, CHUNK), const2),
                  pl.BlockSpec((CHUNK, CHUNK), const2)],
        out_specs=tok,
        out_shape=jax.ShapeDtypeStruct((B, S, MIX), BF16),
        scratch_shapes=[pltpu.VMEM((N_HEADS, HEAD, HEAD), F32)],
        compiler_params=pltpu.CompilerParams(dimension_semantics=("arbitrary", "arbitrary")),
        name="hgrn2",
    )(hq, kin, logf, hi, sg, gain, tril, lvl)


def _attn_kernel(lam_ref, q_ref, k_ref, vt_ref, sdg_ref, gain_ref, o_ref,
                 m_ref, acc_ref, s0_ref, s1_ref):
    tq = q_ref.shape[0]
    tk = vt_ref.shape[-1]
    i = pl.program_id(2)
    q = q_ref[...].astype(F32)
    lane = lax.broadcasted_iota(jnp.int32, q.shape, 1)
    qz = jnp.concatenate([jnp.where(lane < DIFF_HEAD_DIM, q, 0.0),
                          jnp.where(lane >= DIFF_HEAD_DIM, q, 0.0)], axis=0)
    qzt = qz.T.astype(BF16)
    ones_rows = jnp.ones((ONES_ROWS, tk), BF16)

    m_ref[...] = jnp.full(m_ref.shape, -jnp.inf, F32)
    acc_ref[...] = jnp.zeros(acc_ref.shape, F32)

    def scores(j, s_ref):
        kb = k_ref[pl.ds(pl.multiple_of(j * tk, tk), tk), :]
        s_ref[...] = jnp.dot(kb, qzt, preferred_element_type=F32)

    def accumulate(j, s_ref, diag_part):
        s = s_ref[...]
        if diag_part is not None:
            key = lax.broadcasted_iota(jnp.int32, s.shape, 0) + diag_part * tk
            col = lax.broadcasted_iota(jnp.int32, s.shape, 1)
            qry = jnp.where(col >= tq, col - tq, col)
            s = jnp.where(key <= qry, s, -jnp.inf)
        m_old = m_ref[...]
        m_new = jnp.maximum(m_old, jnp.max(s, axis=0, keepdims=True))
        alpha = jnp.exp2(m_old - m_new)
        p = jnp.exp2(s - m_new).astype(BF16)
        vt = jnp.concatenate([vt_ref[j], ones_rows], axis=0)
        acc_ref[...] = alpha * acc_ref[...] + jnp.dot(vt, p, preferred_element_type=F32)
        m_ref[...] = m_new

    scores(0, s0_ref)

    def body(jj, carry):
        scores(2 * jj + 1, s1_ref)
        accumulate(2 * jj, s0_ref, None)
        scores(2 * jj + 2, s0_ref)
        accumulate(2 * jj + 1, s1_ref, None)
        return carry

    lax.fori_loop(0, i, body, 0)
    scores(2 * i + 1, s1_ref)
    accumulate(2 * i, s0_ref, 0)
    accumulate(2 * i + 1, s1_ref, 1)

    acc = acc_ref[...]
    o_t = acc[0:HEAD, :] * (1.0 / acc[HEAD:HEAD + 1, :])
    o = (o_t[:, :tq] - lam_ref[0] * o_t[:, tq:]).T
    o = o * lax.rsqrt(jnp.mean(o * o, axis=-1, keepdims=True) + RMS_EPS) * gain_ref[...]
    o_ref[...] = (o * (1.0 - LAM_INIT) * sdg_ref[...].astype(F32)).astype(BF16)


def _attn(lam, dq, dk, dvt, sdg, gain, tq):
    B, S, _ = dq.shape
    nkb, tk = dvt.shape[1], dvt.shape[3]
    assert tq == 2 * tk, "the diagonal of a query block is covered by exactly two key sub-blocks"
    qblk = pl.BlockSpec((None, tq, HEAD), lambda b, h, i: (b, i, h))
    return pl.pallas_call(
        _attn_kernel,
        grid=(B, N_HEADS, S // tq),
        in_specs=[
            pl.BlockSpec(memory_space=pltpu.SMEM),
            qblk,
            pl.BlockSpec((None, S, HEAD), lambda b, h, i: (b, 0, h)),
            pl.BlockSpec((None, nkb, HEAD, tk), lambda b, h, i: (b, 0, h, 0)),
            qblk,
            pl.BlockSpec((1, HEAD), lambda b, h, i: (0, h)),
        ],
        out_specs=qblk,
        out_shape=jax.ShapeDtypeStruct((B, S, MIX), BF16),
        scratch_shapes=[pltpu.VMEM((1, 2 * tq), F32),
                        pltpu.VMEM((HEAD + ONES_ROWS, 2 * tq), F32),
                        pltpu.VMEM((tk, 2 * tq), F32),
                        pltpu.VMEM((tk, 2 * tq), F32)],
        compiler_params=pltpu.CompilerParams(
            dimension_semantics=("parallel", "parallel", "arbitrary"),
            vmem_limit_bytes=V7X_VMEM_BYTES * 3 // 4),
        name="attn",
    )(lam, dq, dk, dvt, sdg, gain)


def _outproj_kernel(oa_ref, ob_ref, gma_ref, gmb_ref, x_ref, mod_ref, gpost_ref,
                    wa_ref, wb_ref, wo_ref, out_ref):
    ya = jnp.dot(oa_ref[...], wa_ref[...], preferred_element_type=F32)
    yb = jnp.dot(ob_ref[...], wb_ref[...], preferred_element_type=F32)
    y = gma_ref[...].astype(F32) * ya + gmb_ref[...].astype(F32) * yb
    y = jnp.dot(y.astype(BF16), wo_ref[...], preferred_element_type=F32)
    yn = y * lax.rsqrt(jnp.mean(y * y, axis=-1, keepdims=True) + RMS_EPS) * gpost_ref[...]
    out_ref[...] = x_ref[...] + mod_ref[2:3, :] * yn


def _outproj(oa, ob, gma, gmb, x, mod3, g_post, wa, wb, wo, tm):
    B, S, D = x.shape
    tok = lambda width: pl.BlockSpec((None, tm, width), lambda b, i: (b, i, 0))
    const2 = lambda b, i: (0, 0)
    return pl.pallas_call(
        _outproj_kernel,
        grid=(B, S // tm),
        in_specs=[
            tok(MIX), tok(MIX), tok(D), tok(D), tok(D),
            pl.BlockSpec((None, 3, D), lambda b, i: (b, 0, 0)),
            pl.BlockSpec((1, D), const2),
            pl.BlockSpec((MIX, D), const2),
            pl.BlockSpec((MIX, D), const2),
            pl.BlockSpec((D, D), const2),
        ],
        out_specs=tok(D),
        out_shape=jax.ShapeDtypeStruct((B, S, D), x.dtype),
        compiler_params=pltpu.CompilerParams(
            dimension_semantics=("parallel", "parallel"),
            vmem_limit_bytes=V7X_VMEM_BYTES * 3 // 4),
        name="outproj",
    )(oa, ob, gma, gmb, x, mod3, g_post, wa, wb, wo)


def kernel(x, c, w_ada, b_ada, g_pre, g_post, w_in, lb_logits, hg_norm_gain,
           lambda_q1, lambda_k1, lambda_q2, lambda_k2, diff_norm_gain,
           w_branch_a, w_branch_b, w_out):
    B, S, D = x.shape
    assert w_in.shape[0] == 1 and lb_logits.shape[0] == 2, "single-layer block"
    assert w_in.shape[2] == 8 * MIX + 2 * D
    tm = min(512, S)
    tc = min(256, S)
    assert S % tm == 0 and S % tc == 0 and tc % CHUNK == 0

    mod, lbv, lam = _prep(c, w_ada[0], b_ada, lb_logits, lambda_q1, lambda_k1, lambda_q2, lambda_k2)
    mod3 = mod.reshape(B, 3, D)

    hq, kin, logf, hi, sg, dq, dk, dvt, sdg, gma, gmb = _inproj(
        x, mod3, g_pre, lbv, w_in[0].astype(BF16), tm, tm // 2)

    o_a = _hgrn2(hq, kin, logf, hi, sg, hg_norm_gain, tc)
    o_b = _attn(lam[0, :1], dq, dk, dvt, sdg, diff_norm_gain, tm)

    return _outproj(o_a, o_b, gma, gmb, x, mod3, g_post,
                    w_branch_a[0].astype(BF16), w_branch_b[0].astype(BF16), w_out[0].astype(BF16), tm)
```

```python
import functools
import math

import numpy as np
import jax
import jax.numpy as jnp
from jax import lax
from jax.experimental import pallas as pl
from jax.experimental.pallas import tpu as pltpu

F32 = jnp.float32
BF16 = jnp.bfloat16
RMS_EPS = 1e-6
LOG2E = 1.4426950408889634

HEAD = 128
DIFF_HEAD_DIM = 64
N_HEADS = 4
MIX = N_HEADS * HEAD
CHUNK = 128
LEVELS = (2, 4, 8, 16, 32, 64)
CODE_SUB = 1 + len(LEVELS)
CODE_DIAG = 2 + len(LEVELS)
LAM_INIT = 0.8 - 0.6 * math.exp(-0.3 * 0)
ONES_ROWS = 16

V7X_VMEM_BYTES = 64 * 1024 * 1024
NT_DIMS = (((1,), (1,)), ((), ()))


def _sigmoid(z):
    return jax.nn.sigmoid(z)


def _prep_kernel(c_ref, w_ref, b_ref, lbl_ref, lq1_ref, lk1_ref, lq2_ref, lk2_ref,
                 mod_ref, lb_ref, lam_ref):
    c = c_ref[...]
    sc = c * _sigmoid(c)
    mod_ref[...] = jnp.dot(sc, w_ref[...], precision=lax.Precision.HIGHEST,
                           preferred_element_type=F32) + b_ref[...]
    z = lbl_ref[...]
    e = jnp.exp(z - jnp.max(z, axis=0, keepdims=True))
    lb_ref[...] = e / jnp.sum(e, axis=0, keepdims=True)
    d1 = jnp.sum(lq1_ref[...] * lk1_ref[...], axis=-1, keepdims=True)
    d2 = jnp.sum(lq2_ref[...] * lk2_ref[...], axis=-1, keepdims=True)
    lam = jnp.exp(d1) - jnp.exp(d2) + LAM_INIT
    lam_ref[...] = jnp.broadcast_to(lam, lam_ref.shape)


def _prep(c, w_ada, b_ada, lb_logits, lq1, lk1, lq2, lk2):
    B, D = c.shape
    n_tiles = w_ada.shape[1] // D
    const = lambda j: (0, 0)
    return pl.pallas_call(
        _prep_kernel,
        grid=(n_tiles,),
        in_specs=[
            pl.BlockSpec((B, D), const),
            pl.BlockSpec((D, D), lambda j: (0, j)),
            pl.BlockSpec((1, D), lambda j: (0, j)),
            pl.BlockSpec(lb_logits.shape, const),
            pl.BlockSpec(lq1.shape, const),
            pl.BlockSpec(lk1.shape, const),
            pl.BlockSpec(lq2.shape, const),
            pl.BlockSpec(lk2.shape, const),
        ],
        out_specs=[
            pl.BlockSpec((B, D), lambda j: (0, j)),
            pl.BlockSpec(lb_logits.shape, const),
            pl.BlockSpec((1, HEAD), const),
        ],
        out_shape=[
            jax.ShapeDtypeStruct((B, n_tiles * D), F32),
            jax.ShapeDtypeStruct(lb_logits.shape, F32),
            jax.ShapeDtypeStruct((1, HEAD), F32),
        ],
        compiler_params=pltpu.CompilerParams(dimension_semantics=("arbitrary",)),
        name="prep",
    )(c, w_ada, b_ada, lb_logits, lq1, lk1, lq2, lk2)


def _inproj_kernel(x_ref, mod_ref, gpre_ref, lb_ref, w_ref,
                   hq_ref, kin_ref, logf_ref, hi_ref, sg_ref,
                   dqt_ref, dk_ref, dvt_ref, sdg_ref, gma_ref, gmb_ref):
    x = x_ref[...]
    ms = jnp.mean(x * x, axis=-1, keepdims=True)
    xn = x * lax.rsqrt(ms + RMS_EPS) * gpre_ref[...]
    h = xn * (1.0 + mod_ref[1:2, :]) + mod_ref[0:1, :]
    hb = h.astype(BF16)

    def proj(c0, width):
        return jnp.dot(hb, w_ref[:, c0:c0 + width], preferred_element_type=F32)

    hq_ref[...] = proj(0 * MIX, MIX).astype(BF16)
    sig = _sigmoid(proj(1 * MIX, MIX))
    lb = lb_ref[0:1, :]
    oml = lb_ref[1:2, :]
    logf_ref[...] = jnp.log(lb + oml * sig)
    kin_ref[...] = (oml * (1.0 - sig)).astype(BF16)
    hi_ref[...] = proj(2 * MIX, MIX).astype(BF16)
    z = proj(3 * MIX, MIX)
    sg_ref[...] = (z * _sigmoid(z)).astype(BF16)
    dqt_ref[...] = (proj(4 * MIX, MIX) * (DIFF_HEAD_DIM ** -0.5 * LOG2E)).T.astype(BF16)
    dk_ref[...] = proj(5 * MIX, MIX).astype(BF16)
    dv_t = proj(6 * MIX, MIX).T.astype(BF16)
    tk = dvt_ref.shape[-1]
    for u in range(dvt_ref.shape[0]):
        dvt_ref[u] = dv_t[:, u * tk:(u + 1) * tk]
    z = proj(7 * MIX, MIX)
    sdg_ref[...] = (z * _sigmoid(z)).astype(BF16)
    d_model = gma_ref.shape[-1]
    gma_ref[...] = _sigmoid(proj(8 * MIX, d_model)).astype(BF16)
    gmb_ref[...] = _sigmoid(proj(8 * MIX + d_model, d_model)).astype(BF16)


def _inproj(x, mod3, g_pre, lbv, w_in_bf, tm, tk):
    B, S, D = x.shape
    n_in = w_in_bf.shape[1]
    nt = S // tm
    ku = tm // tk
    tok = lambda width: pl.BlockSpec((None, tm, width), lambda b, i: (b, i, 0))
    const2 = lambda b, i: (0, 0)
    sds = lambda width, dt: jax.ShapeDtypeStruct((B, S, width), dt)
    return pl.pallas_call(
        _inproj_kernel,
        grid=(B, nt),
        in_specs=[
            tok(D),
            pl.BlockSpec((None, 3, D), lambda b, i: (b, 0, 0)),
            pl.BlockSpec((1, D), const2),
            pl.BlockSpec(lbv.shape, const2),
            pl.BlockSpec((D, n_in), const2, pipeline_mode=pl.Buffered(1)),
        ],
        out_specs=[
            tok(MIX), tok(MIX), tok(MIX), tok(MIX), tok(MIX),
            pl.BlockSpec((None, None, MIX, tm), lambda b, i: (b, i, 0, 0)),
            tok(MIX),
            pl.BlockSpec((None, ku, MIX, tk), lambda b, i: (b, i, 0, 0)),
            tok(MIX), tok(D), tok(D),
        ],
        out_shape=[
            sds(MIX, BF16), sds(MIX, BF16), sds(MIX, F32), sds(MIX, BF16), sds(MIX, BF16),
            jax.ShapeDtypeStruct((B, nt, MIX, tm), BF16),
            sds(MIX, BF16),
            jax.ShapeDtypeStruct((B, nt * ku, MIX, tk), BF16),
            sds(MIX, BF16), sds(D, BF16), sds(D, BF16),
        ],
        compiler_params=pltpu.CompilerParams(
            dimension_semantics=("parallel", "parallel"),
            vmem_limit_bytes=V7X_VMEM_BYTES * 7 // 8),
        name="inproj",
    )(x, mod3, g_pre, lbv, w_in_bf)


def _level_codes():
    t = np.arange(CHUNK)[:, None]
    s = np.arange(CHUNK)[None, :]
    code = np.zeros((CHUNK, CHUNK), np.int32)
    for i, m in reversed(list(enumerate(LEVELS))):
        code = np.where((t // (2 * m) == s // (2 * m)) & (s < t), 1 + i, code)
    code = np.where((t // 2 == s // 2) & (s == t - 1), CODE_SUB, code)
    code = np.where(s == t, CODE_DIAG, code)
    return code.astype(np.int32)


def _split3(x):
    h1 = x.astype(BF16)
    r1 = x - h1.astype(F32)
    h2 = r1.astype(BF16)
    h3 = (r1 - h2.astype(F32)).astype(BF16)
    return h1, h2, h3


def _block_row(a, block, row):
    if block < 8:
        pos = lax.broadcasted_iota(jnp.int32, a.shape, 0) % block
        out = a
        for r in range(block):
            if r != row:
                out = jnp.where(pos == r, pltpu.roll(a, (r - row) % CHUNK, axis=0), out)
        return out
    g = CHUNK // block
    a3 = a.reshape(g, block, HEAD)
    return jnp.broadcast_to(a3[:, row:row + 1, :], (g, block, HEAD)).reshape(CHUNK, HEAD)


def _hgrn2_kernel(hq_ref, kin_ref, logf_ref, hi_ref, sg_ref, gain_ref, tril_ref, lvl_ref,
                  out_ref, state_ref):
    @pl.when(pl.program_id(1) == 0)
    def _():
        state_ref[...] = jnp.zeros_like(state_ref)

    tril = tril_ref[...]
    lvl = lvl_ref[...]
    ones = jnp.ones((HEAD, HEAD), BF16)
    n_chunks = hq_ref.shape[0] // CHUNK
    for ci in range(n_chunks):
        rows = slice(ci * CHUNK, (ci + 1) * CHUNK)
        lf_all = logf_ref[rows, :] * LOG2E
        l1, l2, l3 = _split3(lf_all)
        b_all = (jnp.dot(tril, l3, preferred_element_type=F32)
                 + jnp.dot(tril, l2, preferred_element_type=F32)
                 + jnp.dot(tril, l1, preferred_element_type=F32))
        for h in range(N_HEADS):
            cols = slice(h * HEAD, (h + 1) * HEAD)
            b = b_all[:, cols]
            q = hq_ref[rows, cols].astype(F32)
            k = kin_ref[rows, cols].astype(F32)
            vb = hi_ref[rows, cols]
            b_last = b[CHUNK - 1:CHUNK, :]

            state_t = state_ref[h]
            qbar = (q * jnp.exp2(b)).astype(BF16)
            o = lax.dot_general(qbar, state_t.astype(BF16), NT_DIMS, preferred_element_type=F32)
            kbar = (k * jnp.exp2(b_last - b)).astype(BF16)
            v_t = vb.astype(F32).T.astype(BF16)
            state_ref[h] = state_t * jnp.exp2(b_last) + jnp.dot(v_t, kbar, preferred_element_type=F32)

            k_prev = pltpu.roll(k, 1, axis=0)
            p_diag = (q * k).astype(BF16)
            p_sub = (q * k_prev * jnp.exp2(lf_all[:, cols])).astype(BF16)
            scores = jnp.where(lvl == CODE_DIAG, jnp.dot(p_diag, ones, preferred_element_type=F32), 0.0)
            scores = jnp.where(lvl == CODE_SUB, jnp.dot(p_sub, ones, preferred_element_type=F32), scores)
            for i, m in enumerate(LEVELS):
                e = jnp.exp2(b - _block_row(b, 2 * m, m - 1))
                qm = (q * e).astype(BF16)
                km = (k * (1.0 / e)).astype(BF16)
                am = lax.dot_general(qm, km, NT_DIMS, preferred_element_type=F32)
                scores = jnp.where(lvl == 1 + i, am, scores)
            o = o + jnp.dot(scores.astype(BF16), vb, preferred_element_type=F32)

            o = o * lax.rsqrt(jnp.mean(o * o, axis=-1, keepdims=True) + RMS_EPS) * gain_ref[:, cols]
            out_ref[rows, cols] = (o * sg_ref[rows, cols].astype(F32)).astype(BF16)


def _hgrn2(hq, kin, logf, hi, sg, gain, tc):
    B, S, _ = hq.shape
    tril = jnp.asarray(np.tril(np.ones((CHUNK, CHUNK), np.float32)), BF16)
    lvl = jnp.asarray(_level_codes())
    tok = pl.BlockSpec((None, tc, MIX), lambda b, i: (b, i, 0))
    const2 = lambda b, i: (0, 0)
    return pl.pallas_call(
        _hgrn2_kernel,
        grid=(B, S // tc),
        in_specs=[tok, tok, tok, tok, tok,
                  pl.BlockSpec((1, MIX), const2),
                  pl.BlockSpec((CHUNK, CHUNK), const2),
                  pl.BlockSpec((CHUNK, CHUNK), const2)],
        out_specs=tok,
        out_shape=jax.ShapeDtypeStruct((B, S, MIX), BF16),
        scratch_shapes=[pltpu.VMEM((N_HEADS, HEAD, HEAD), F32)],
        compiler_params=pltpu.CompilerParams(dimension_semantics=("arbitrary", "arbitrary")),
        name="hgrn2",
    )(hq, kin, logf, hi, sg, gain, tril, lvl)


def _attn_kernel(lam_ref, qt_ref, k_ref, vt_ref, sdg_ref, gain_ref, o_ref,
                 m_ref, acc_ref, s0_ref, s1_ref, qzt_ref):
    nq, _, tq = qt_ref.shape
    tk = vt_ref.shape[-1]
    i = pl.program_id(2)
    ones_rows = jnp.ones((ONES_ROWS, tk), BF16)

    def first_scores(qi):
        qt = qt_ref[qi].astype(F32)
        row = lax.broadcasted_iota(jnp.int32, qt.shape, 0)
        qzt_ref[...] = jnp.concatenate([jnp.where(row < DIFF_HEAD_DIM, qt, 0.0),
                                        jnp.where(row >= DIFF_HEAD_DIM, qt, 0.0)], axis=1).astype(BF16)
        s0_ref[...] = jnp.dot(k_ref[0:tk, :], qzt_ref[...], preferred_element_type=F32)

    @pl.when(i == 0)
    def _():
        first_scores(0)

    m_ref[...] = jnp.full(m_ref.shape, -jnp.inf, F32)
    acc_ref[...] = jnp.zeros(acc_ref.shape, F32)

    def key_block(j):
        return k_ref[pl.ds(pl.multiple_of(j * tk, tk), tk), :]

    def scores(j, s_ref):
        s_ref[...] = jnp.dot(key_block(j), qzt_ref[...], preferred_element_type=F32)

    def accumulate(j, s, lanes=slice(None)):
        m_old = m_ref[:, lanes]
        m_new = jnp.maximum(m_old, jnp.max(s, axis=0, keepdims=True))
        alpha = jnp.exp2(m_old - m_new)
        p = jnp.exp2(s - m_new).astype(BF16)
        vt = jnp.concatenate([vt_ref[j], ones_rows], axis=0)
        acc_ref[:, lanes] = alpha * acc_ref[:, lanes] + jnp.dot(vt, p, preferred_element_type=F32)
        m_ref[:, lanes] = m_new

    def pair(jj):
        scores(2 * jj + 1, s1_ref)
        accumulate(2 * jj, s0_ref[...])
        scores(2 * jj + 2, s0_ref)
        accumulate(2 * jj + 1, s1_ref[...])

    def two_pairs(t, carry):
        pair(2 * t)
        pair(2 * t + 1)
        return carry

    def one_pair(t, carry):
        pair(i - 1)
        return carry

    lax.fori_loop(0, i // 2, two_pairs, 0)
    lax.fori_loop(0, i % 2, one_pair, 0)

    kb_last = key_block(2 * i + 1)
    upper = [slice(mi * tq + tk, (mi + 1) * tq) for mi in range(2)]
    for lanes in upper:
        s1_ref[:, lanes] = jnp.dot(kb_last, qzt_ref[:, lanes], preferred_element_type=F32)

    s = s0_ref[...]
    key = lax.broadcasted_iota(jnp.int32, s.shape, 0)
    col = lax.broadcasted_iota(jnp.int32, s.shape, 1)
    qry = jnp.where(col >= tq, col - tq, col)
    accumulate(2 * i, jnp.where(key <= qry, s, -jnp.inf))

    first_scores(jnp.minimum(i + 1, nq - 1))

    key = lax.broadcasted_iota(jnp.int32, (tk, tk), 0)
    qry = lax.broadcasted_iota(jnp.int32, (tk, tk), 1)
    for lanes in upper:
        accumulate(2 * i + 1, jnp.where(key <= qry, s1_ref[:, lanes], -jnp.inf), lanes)

    acc = acc_ref[...]
    o_t = acc[0:HEAD, :] * (1.0 / acc[HEAD:HEAD + 1, :])
    o = (o_t[:, :tq] - lam_ref[0] * o_t[:, tq:]).T
    o = o * lax.rsqrt(jnp.mean(o * o, axis=-1, keepdims=True) + RMS_EPS) * gain_ref[...]
    o_ref[...] = (o * (1.0 - LAM_INIT) * sdg_ref[...].astype(F32)).astype(BF16)


def _attn(lam, dqt, dk, dvt, sdg, gain):
    B, S, _ = dk.shape
    nq, tq = dqt.shape[1], dqt.shape[3]
    nkb, tk = dvt.shape[1], dvt.shape[3]
    assert tq == 2 * tk, "the diagonal of a query block is covered by exactly two key sub-blocks"
    qblk = pl.BlockSpec((None, tq, HEAD), lambda b, h, i: (b, i, h))
    return pl.pallas_call(
        _attn_kernel,
        grid=(B, N_HEADS, nq),
        in_specs=[
            pl.BlockSpec(memory_space=pltpu.SMEM),
            pl.BlockSpec((None, nq, HEAD, tq), lambda b, h, i: (b, 0, h, 0)),
            pl.BlockSpec((None, S, HEAD), lambda b, h, i: (b, 0, h)),
            pl.BlockSpec((None, nkb, HEAD, tk), lambda b, h, i: (b, 0, h, 0)),
            qblk,
            pl.BlockSpec((1, HEAD), lambda b, h, i: (0, h)),
        ],
        out_specs=qblk,
        out_shape=jax.ShapeDtypeStruct((B, S, MIX), BF16),
        scratch_shapes=[pltpu.VMEM((1, 2 * tq), F32),
                        pltpu.VMEM((HEAD + ONES_ROWS, 2 * tq), F32),
                        pltpu.VMEM((tk, 2 * tq), F32),
                        pltpu.VMEM((tk, 2 * tq), F32),
                        pltpu.VMEM((HEAD, 2 * tq), BF16)],
        compiler_params=pltpu.CompilerParams(
            dimension_semantics=("arbitrary", "arbitrary", "arbitrary"),
            vmem_limit_bytes=V7X_VMEM_BYTES * 3 // 4),
        name="attn",
    )(lam, dqt, dk, dvt, sdg, gain)


def _outproj_kernel(oa_ref, ob_ref, gma_ref, gmb_ref, x_ref, mod_ref, gpost_ref,
                    wa_ref, wb_ref, wo_ref, out_ref):
    ya = jnp.dot(oa_ref[...], wa_ref[...], preferred_element_type=F32)
    yb = jnp.dot(ob_ref[...], wb_ref[...], preferred_element_type=F32)
    y = gma_ref[...].astype(F32) * ya + gmb_ref[...].astype(F32) * yb
    y = jnp.dot(y.astype(BF16), wo_ref[...], preferred_element_type=F32)
    yn = y * lax.rsqrt(jnp.mean(y * y, axis=-1, keepdims=True) + RMS_EPS) * gpost_ref[...]
    out_ref[...] = x_ref[...] + mod_ref[2:3, :] * yn


def _outproj(oa, ob, gma, gmb, x, mod3, g_post, wa, wb, wo, tm):
    B, S, D = x.shape
    tok = lambda width: pl.BlockSpec((None, tm, width), lambda b, i: (b, i, 0))
    const2 = lambda b, i: (0, 0)
    return pl.pallas_call(
        _outproj_kernel,
        grid=(B, S // tm),
        in_specs=[
            tok(MIX), tok(MIX), tok(D), tok(D), tok(D),
            pl.BlockSpec((None, 3, D), lambda b, i: (b, 0, 0)),
            pl.BlockSpec((1, D), const2),
            pl.BlockSpec((MIX, D), const2),
            pl.BlockSpec((MIX, D), const2),
            pl.BlockSpec((D, D), const2),
        ],
        out_specs=tok(D),
        out_shape=jax.ShapeDtypeStruct((B, S, D), x.dtype),
        compiler_params=pltpu.CompilerParams(
            dimension_semantics=("parallel", "parallel"),
            vmem_limit_bytes=V7X_VMEM_BYTES * 3 // 4),
        name="outproj",
    )(oa, ob, gma, gmb, x, mod3, g_post, wa, wb, wo)


def kernel(x, c, w_ada, b_ada, g_pre, g_post, w_in, lb_logits, hg_norm_gain,
           lambda_q1, lambda_k1, lambda_q2, lambda_k2, diff_norm_gain,
           w_branch_a, w_branch_b, w_out):
    B, S, D = x.shape
    assert w_in.shape[0] == 1 and lb_logits.shape[0] == 2, "single-layer block"
    assert w_in.shape[2] == 8 * MIX + 2 * D
    tm = min(512, S)
    tc = min(256, S)
    assert S % tm == 0 and S % tc == 0 and tc % CHUNK == 0

    mod, lbv, lam = _prep(c, w_ada[0], b_ada, lb_logits, lambda_q1, lambda_k1, lambda_q2, lambda_k2)
    mod3 = mod.reshape(B, 3, D)

    hq, kin, logf, hi, sg, dqt, dk, dvt, sdg, gma, gmb = _inproj(
        x, mod3, g_pre, lbv, w_in[0].astype(BF16), tm, tm // 2)

    o_a = _hgrn2(hq, kin, logf, hi, sg, hg_norm_gain, tc)
    o_b = _attn(lam[0, :1], dqt, dk, dvt, sdg, diff_norm_gain)

    return _outproj(o_a, o_b, gma, gmb, x, mod3, g_post,
                    w_branch_a[0].astype(BF16), w_branch_b[0].astype(BF16), w_out[0].astype(BF16), tm)
```

```python
import functools
import math

import numpy as np
import jax
import jax.numpy as jnp
from jax import lax
from jax.experimental import pallas as pl
from jax.experimental.pallas import tpu as pltpu

F32 = jnp.float32
BF16 = jnp.bfloat16
RMS_EPS = 1e-6
LOG2E = 1.4426950408889634

HEAD = 128
DIFF_HEAD_DIM = 64
N_HEADS = 4
MIX = N_HEADS * HEAD
CHUNK = 128
PROJ_SLICE = 256
LEVELS = (2, 4, 8, 16, 32, 64)
CODE_SUB = 1 + len(LEVELS)
CODE_DIAG = 2 + len(LEVELS)
LAM_INIT = 0.8 - 0.6 * math.exp(-0.3 * 0)
ONES_ROWS = 16

V7X_VMEM_BYTES = 64 * 1024 * 1024
NT_DIMS = (((1,), (1,)), ((), ()))


def _sigmoid(z):
    return jax.nn.sigmoid(z)


def _prep_kernel(c_ref, w_ref, b_ref, lbl_ref, lq1_ref, lk1_ref, lq2_ref, lk2_ref,
                 mod_ref, lb_ref, lam_ref):
    c = c_ref[...]
    sc = c * _sigmoid(c)
    mod_ref[...] = jnp.dot(sc, w_ref[...], precision=lax.Precision.HIGHEST,
                           preferred_element_type=F32) + b_ref[...]
    z = lbl_ref[...]
    e = jnp.exp(z - jnp.max(z, axis=0, keepdims=True))
    lb_ref[...] = e / jnp.sum(e, axis=0, keepdims=True)
    d1 = jnp.sum(lq1_ref[...] * lk1_ref[...], axis=-1, keepdims=True)
    d2 = jnp.sum(lq2_ref[...] * lk2_ref[...], axis=-1, keepdims=True)
    lam = jnp.exp(d1) - jnp.exp(d2) + LAM_INIT
    lam_ref[...] = jnp.broadcast_to(lam, lam_ref.shape)


def _prep(c, w_ada, b_ada, lb_logits, lq1, lk1, lq2, lk2):
    B, D = c.shape
    n_tiles = w_ada.shape[1] // D
    const = lambda j: (0, 0)
    return pl.pallas_call(
        _prep_kernel,
        grid=(n_tiles,),
        in_specs=[
            pl.BlockSpec((B, D), const),
            pl.BlockSpec((D, D), lambda j: (0, j)),
            pl.BlockSpec((1, D), lambda j: (0, j)),
            pl.BlockSpec(lb_logits.shape, const),
            pl.BlockSpec(lq1.shape, const),
            pl.BlockSpec(lk1.shape, const),
            pl.BlockSpec(lq2.shape, const),
            pl.BlockSpec(lk2.shape, const),
        ],
        out_specs=[
            pl.BlockSpec((B, D), lambda j: (0, j)),
            pl.BlockSpec(lb_logits.shape, const),
            pl.BlockSpec((1, HEAD), const),
        ],
        out_shape=[
            jax.ShapeDtypeStruct((B, n_tiles * D), F32),
            jax.ShapeDtypeStruct(lb_logits.shape, F32),
            jax.ShapeDtypeStruct((1, HEAD), F32),
        ],
        compiler_params=pltpu.CompilerParams(dimension_semantics=("arbitrary",)),
        name="prep",
    )(c, w_ada, b_ada, lb_logits, lq1, lk1, lq2, lk2)


def _inproj_kernel(x_ref, mod_ref, gpre_ref, lb_ref, w_ref, gain_ref, tril_ref, lvl_ref,
                   oa_ref, dqt_ref, dk_ref, dvt_ref, sdg_ref, gma_ref, gmb_ref,
                   hq_ref, kin_ref, logf_ref, hi_ref, sg_ref, state_ref):
    x = x_ref[...]
    ms = jnp.mean(x * x, axis=-1, keepdims=True)
    xn = x * lax.rsqrt(ms + RMS_EPS) * gpre_ref[...]
    h = xn * (1.0 + mod_ref[1:2, :]) + mod_ref[0:1, :]
    hb = h.astype(BF16)

    def proj(c0, width):
        return jnp.dot(hb, w_ref[:, c0:c0 + width], preferred_element_type=F32)

    hq_ref[...] = proj(0 * MIX, MIX).astype(BF16)
    sig = _sigmoid(proj(1 * MIX, MIX))
    lb = lb_ref[0:1, :]
    oml = lb_ref[1:2, :]
    logf_ref[...] = jnp.log(lb + oml * sig)
    kin_ref[...] = (oml * (1.0 - sig)).astype(BF16)
    hi_ref[...] = proj(2 * MIX, MIX).astype(BF16)
    z = proj(3 * MIX, MIX)
    sg_ref[...] = (z * _sigmoid(z)).astype(BF16)
    tk = dvt_ref.shape[-1]
    d_model = gma_ref.shape[-1]

    def q_slice(c):
        z = proj(4 * MIX + c, PROJ_SLICE) * (DIFF_HEAD_DIM ** -0.5 * LOG2E)
        dqt_ref[c:c + PROJ_SLICE, :] = z.T.astype(BF16)

    def k_slice(c):
        dk_ref[:, c:c + PROJ_SLICE] = proj(5 * MIX + c, PROJ_SLICE).astype(BF16)

    def v_slice(c):
        z_t = proj(6 * MIX + c, PROJ_SLICE).T.astype(BF16)
        for u in range(dvt_ref.shape[0]):
            dvt_ref[u, c:c + PROJ_SLICE, :] = z_t[:, u * tk:(u + 1) * tk]

    def g_slice(c):
        z = proj(7 * MIX + c, PROJ_SLICE)
        sdg_ref[:, c:c + PROJ_SLICE] = (z * _sigmoid(z)).astype(BF16)

    def ma_slice(c):
        gma_ref[:, c:c + PROJ_SLICE] = _sigmoid(proj(8 * MIX + c, PROJ_SLICE)).astype(BF16)

    def mb_slice(c):
        gmb_ref[:, c:c + PROJ_SLICE] = _sigmoid(proj(8 * MIX + d_model + c, PROJ_SLICE)).astype(BF16)

    pieces = [functools.partial(fn, c)
              for fn, width in ((q_slice, MIX), (k_slice, MIX), (v_slice, MIX), (g_slice, MIX),
                                (ma_slice, d_model), (mb_slice, d_model))
              for c in range(0, width, PROJ_SLICE)]
    _hgrn2_tile(hq_ref, kin_ref, logf_ref, hi_ref, sg_ref, gain_ref, tril_ref, lvl_ref,
                oa_ref, state_ref, pieces)


def _inproj(x, mod3, g_pre, lbv, w_in_bf, gain_a, tm, tk):
    B, S, D = x.shape
    n_in = w_in_bf.shape[1]
    nt = S // tm
    ku = tm // tk
    tril = jnp.asarray(np.tril(np.ones((CHUNK, CHUNK), np.float32)), BF16)
    lvl = jnp.asarray(_level_codes())
    tok = lambda width: pl.BlockSpec((None, tm, width), lambda b, i: (b, i, 0))
    const2 = lambda b, i: (0, 0)
    sds = lambda width, dt: jax.ShapeDtypeStruct((B, S, width), dt)
    return pl.pallas_call(
        _inproj_kernel,
        grid=(B, nt),
        in_specs=[
            tok(D),
            pl.BlockSpec((None, 3, D), lambda b, i: (b, 0, 0)),
            pl.BlockSpec((1, D), const2),
            pl.BlockSpec(lbv.shape, const2),
            pl.BlockSpec((D, n_in), const2, pipeline_mode=pl.Buffered(1)),
            pl.BlockSpec((1, MIX), const2),
            pl.BlockSpec((CHUNK, CHUNK), const2),
            pl.BlockSpec((CHUNK, CHUNK), const2),
        ],
        out_specs=[
            tok(MIX),
            pl.BlockSpec((None, None, MIX, tm), lambda b, i: (b, i, 0, 0)),
            tok(MIX),
            pl.BlockSpec((None, ku, MIX, tk), lambda b, i: (b, i, 0, 0)),
            tok(MIX), tok(D), tok(D),
        ],
        out_shape=[
            sds(MIX, BF16),
            jax.ShapeDtypeStruct((B, nt, MIX, tm), BF16),
            sds(MIX, BF16),
            jax.ShapeDtypeStruct((B, nt * ku, MIX, tk), BF16),
            sds(MIX, BF16), sds(D, BF16), sds(D, BF16),
        ],
        scratch_shapes=[
            pltpu.VMEM((tm, MIX), BF16), pltpu.VMEM((tm, MIX), BF16), pltpu.VMEM((tm, MIX), F32),
            pltpu.VMEM((tm, MIX), BF16), pltpu.VMEM((tm, MIX), BF16),
            pltpu.VMEM((N_HEADS, HEAD, HEAD), F32),
        ],
        compiler_params=pltpu.CompilerParams(
            dimension_semantics=("arbitrary", "arbitrary"),
            vmem_limit_bytes=V7X_VMEM_BYTES * 7 // 8),
        name="inproj_hgrn2",
    )(x, mod3, g_pre, lbv, w_in_bf, gain_a, tril, lvl)


def _level_codes():
    t = np.arange(CHUNK)[:, None]
    s = np.arange(CHUNK)[None, :]
    code = np.zeros((CHUNK, CHUNK), np.int32)
    for i, m in reversed(list(enumerate(LEVELS))):
        code = np.where((t // (2 * m) == s // (2 * m)) & (s < t), 1 + i, code)
    code = np.where((t // 2 == s // 2) & (s == t - 1), CODE_SUB, code)
    code = np.where(s == t, CODE_DIAG, code)
    return code.astype(np.int32)


def _split3(x):
    h1 = x.astype(BF16)
    r1 = x - h1.astype(F32)
    h2 = r1.astype(BF16)
    h3 = (r1 - h2.astype(F32)).astype(BF16)
    return h1, h2, h3


def _block_row(a, block, row):
    if block < 8:
        pos = lax.broadcasted_iota(jnp.int32, a.shape, 0) % block
        out = a
        for r in range(block):
            if r != row:
                out = jnp.where(pos == r, pltpu.roll(a, (r - row) % CHUNK, axis=0), out)
        return out
    g = CHUNK // block
    a3 = a.reshape(g, block, HEAD)
    return jnp.broadcast_to(a3[:, row:row + 1, :], (g, block, HEAD)).reshape(CHUNK, HEAD)


def _hgrn2_tile(hq_ref, kin_ref, logf_ref, hi_ref, sg_ref, gain_ref, tril_ref, lvl_ref,
                out_ref, state_ref, interleaved=()):
    @pl.when(pl.program_id(1) == 0)
    def _():
        state_ref[...] = jnp.zeros_like(state_ref)

    pending = list(interleaved)
    tril = tril_ref[...]
    lvl = lvl_ref[...]
    ones = jnp.ones((HEAD, HEAD), BF16)
    n_chunks = hq_ref.shape[0] // CHUNK
    for ci in range(n_chunks):
        rows = slice(ci * CHUNK, (ci + 1) * CHUNK)
        lf_all = logf_ref[rows, :] * LOG2E
        l1, l2, l3 = _split3(lf_all)
        b_all = (jnp.dot(tril, l3, preferred_element_type=F32)
                 + jnp.dot(tril, l2, preferred_element_type=F32)
                 + jnp.dot(tril, l1, preferred_element_type=F32))
        for h in range(N_HEADS):
            if pending:
                pending.pop(0)()
            cols = slice(h * HEAD, (h + 1) * HEAD)
            b = b_all[:, cols]
            q = hq_ref[rows, cols].astype(F32)
            k = kin_ref[rows, cols].astype(F32)
            vb = hi_ref[rows, cols]
            b_last = b[CHUNK - 1:CHUNK, :]

            state_t = state_ref[h]
            qbar = (q * jnp.exp2(b)).astype(BF16)
            o = lax.dot_general(qbar, state_t.astype(BF16), NT_DIMS, preferred_element_type=F32)
            kbar = (k * jnp.exp2(b_last - b)).astype(BF16)
            v_t = vb.astype(F32).T.astype(BF16)
            state_ref[h] = state_t * jnp.exp2(b_last) + jnp.dot(v_t, kbar, preferred_element_type=F32)

            k_prev = pltpu.roll(k, 1, axis=0)
            p_diag = (q * k).astype(BF16)
            p_sub = (q * k_prev * jnp.exp2(lf_all[:, cols])).astype(BF16)
            scores = jnp.where(lvl == CODE_DIAG, jnp.dot(p_diag, ones, preferred_element_type=F32), 0.0)
            scores = jnp.where(lvl == CODE_SUB, jnp.dot(p_sub, ones, preferred_element_type=F32), scores)
            for i, m in enumerate(LEVELS):
                e = jnp.exp2(b - _block_row(b, 2 * m, m - 1))
                qm = (q * e).astype(BF16)
                km = (k * (1.0 / e)).astype(BF16)
                am = lax.dot_general(qm, km, NT_DIMS, preferred_element_type=F32)
                scores = jnp.where(lvl == 1 + i, am, scores)
            o = o + jnp.dot(scores.astype(BF16), vb, preferred_element_type=F32)

            o = o * lax.rsqrt(jnp.mean(o * o, axis=-1, keepdims=True) + RMS_EPS) * gain_ref[:, cols]
            out_ref[rows, cols] = (o * sg_ref[rows, cols].astype(F32)).astype(BF16)
    for piece in pending:
        piece()


def _attn_kernel(lam_ref, qt_ref, k_ref, vt_ref, ot_ref,
                 m_ref, acc_ref, s0_ref, s1_ref, qzt_ref):
    nq, _, tq = qt_ref.shape
    tk = vt_ref.shape[-1]
    i = pl.program_id(2)
    ones_rows = jnp.ones((ONES_ROWS, tk), BF16)

    def first_scores(qi):
        qt = qt_ref[qi].astype(F32)
        row = lax.broadcasted_iota(jnp.int32, qt.shape, 0)
        qzt_ref[...] = jnp.concatenate([jnp.where(row < DIFF_HEAD_DIM, qt, 0.0),
                                        jnp.where(row >= DIFF_HEAD_DIM, qt, 0.0)], axis=1).astype(BF16)
        s0_ref[...] = jnp.dot(k_ref[0:tk, :], qzt_ref[...], preferred_element_type=F32)

    @pl.when(i == 0)
    def _():
        first_scores(0)

    m_ref[...] = jnp.full(m_ref.shape, -jnp.inf, F32)
    acc_ref[...] = jnp.zeros(acc_ref.shape, F32)

    def key_block(j):
        return k_ref[pl.ds(pl.multiple_of(j * tk, tk), tk), :]

    def scores(j, s_ref):
        s_ref[...] = jnp.dot(key_block(j), qzt_ref[...], preferred_element_type=F32)

    def accumulate(j, s, lanes=slice(None)):
        m_old = m_ref[:, lanes]
        m_new = jnp.maximum(m_old, jnp.max(s, axis=0, keepdims=True))
        alpha = jnp.exp2(m_old - m_new)
        p = jnp.exp2(s - m_new).astype(BF16)
        vt = jnp.concatenate([vt_ref[j], ones_rows], axis=0)
        acc_ref[:, lanes] = alpha * acc_ref[:, lanes] + jnp.dot(vt, p, preferred_element_type=F32)
        m_ref[:, lanes] = m_new

    def pair(jj):
        scores(2 * jj + 1, s1_ref)
        accumulate(2 * jj, s0_ref[...])
        scores(2 * jj + 2, s0_ref)
        accumulate(2 * jj + 1, s1_ref[...])

    def four_pairs(t, carry):
        for u in range(4):
            pair(4 * t + u)
        return carry

    def two_pairs(t, carry):
        pair(i // 4 * 4)
        pair(i // 4 * 4 + 1)
        return carry

    def one_pair(t, carry):
        pair(i - 1)
        return carry

    lax.fori_loop(0, i // 4, four_pairs, 0)
    lax.fori_loop(0, i % 4 // 2, two_pairs, 0)
    lax.fori_loop(0, i % 2, one_pair, 0)

    kb_last = key_block(2 * i + 1)
    upper = [slice(mi * tq + tk, (mi + 1) * tq) for mi in range(2)]
    for lanes in upper:
        s1_ref[:, lanes] = jnp.dot(kb_last, qzt_ref[:, lanes], preferred_element_type=F32)

    s = s0_ref[...]
    key = lax.broadcasted_iota(jnp.int32, s.shape, 0)
    col = lax.broadcasted_iota(jnp.int32, s.shape, 1)
    qry = jnp.where(col >= tq, col - tq, col)
    accumulate(2 * i, jnp.where(key <= qry, s, -jnp.inf))

    first_scores(jnp.minimum(i + 1, nq - 1))

    key = lax.broadcasted_iota(jnp.int32, (tk, tk), 0)
    qry = lax.broadcasted_iota(jnp.int32, (tk, tk), 1)
    for lanes in upper:
        accumulate(2 * i + 1, jnp.where(key <= qry, s1_ref[:, lanes], -jnp.inf), lanes)

    acc = acc_ref[...]
    o_t = acc[0:HEAD, :] * (1.0 / acc[HEAD:HEAD + 1, :])
    o = o_t[:, :tq] - lam_ref[0] * o_t[:, tq:]
    ot_ref[...] = (o * lax.rsqrt(jnp.mean(o * o, axis=0, keepdims=True) + RMS_EPS)).astype(BF16)


def _attn(lam, dqt, dk, dvt):
    B, S, _ = dk.shape
    nq, tq = dqt.shape[1], dqt.shape[3]
    nkb, tk = dvt.shape[1], dvt.shape[3]
    assert tq == 2 * tk, "the diagonal of a query block is covered by exactly two key sub-blocks"
    return pl.pallas_call(
        _attn_kernel,
        grid=(B, N_HEADS, nq),
        in_specs=[
            pl.BlockSpec(memory_space=pltpu.SMEM),
            pl.BlockSpec((None, nq, HEAD, tq), lambda b, h, i: (b, 0, h, 0)),
            pl.BlockSpec((None, S, HEAD), lambda b, h, i: (b, 0, h)),
            pl.BlockSpec((None, nkb, HEAD, tk), lambda b, h, i: (b, 0, h, 0)),
        ],
        out_specs=pl.BlockSpec((None, HEAD, tq), lambda b, h, i: (b, h, i)),
        out_shape=jax.ShapeDtypeStruct((B, MIX, S), BF16),
        scratch_shapes=[pltpu.VMEM((1, 2 * tq), F32),
                        pltpu.VMEM((HEAD + ONES_ROWS, 2 * tq), F32),
                        pltpu.VMEM((tk, 2 * tq), F32),
                        pltpu.VMEM((tk, 2 * tq), F32),
                        pltpu.VMEM((HEAD, 2 * tq), BF16)],
        compiler_params=pltpu.CompilerParams(
            dimension_semantics=("arbitrary", "arbitrary", "arbitrary"),
            vmem_limit_bytes=V7X_VMEM_BYTES * 3 // 4),
        name="attn",
    )(lam, dqt, dk, dvt)


def _outproj_kernel(oa_ref, obt_ref, sdg_ref, gb_ref, gma_ref, gmb_ref, x_ref, mod_ref, gpost_ref,
                    wa_ref, wb_ref, wo_ref, out_ref):
    ya = jnp.dot(oa_ref[...], wa_ref[...], preferred_element_type=F32)
    ob = obt_ref[...].astype(F32).T * (gb_ref[...] * (1.0 - LAM_INIT)) * sdg_ref[...].astype(F32)
    yb = jnp.dot(ob.astype(BF16), wb_ref[...], preferred_element_type=F32)
    y = gma_ref[...].astype(F32) * ya + gmb_ref[...].astype(F32) * yb
    y = jnp.dot(y.astype(BF16), wo_ref[...], preferred_element_type=F32)
    yn = y * lax.rsqrt(jnp.mean(y * y, axis=-1, keepdims=True) + RMS_EPS) * gpost_ref[...]
    out_ref[...] = x_ref[...] + mod_ref[2:3, :] * yn


def _outproj(oa, obt, sdg, gain_b, gma, gmb, x, mod3, g_post, wa, wb, wo, tm):
    B, S, D = x.shape
    tok = lambda width: pl.BlockSpec((None, tm, width), lambda b, i: (b, i, 0))
    const2 = lambda b, i: (0, 0)
    return pl.pallas_call(
        _outproj_kernel,
        grid=(B, S // tm),
        in_specs=[
            tok(MIX),
            pl.BlockSpec((None, MIX, tm), lambda b, i: (b, 0, i)),
            tok(MIX),
            pl.BlockSpec((1, MIX), const2),
            tok(D), tok(D), tok(D),
            pl.BlockSpec((None, 3, D), lambda b, i: (b, 0, 0)),
            pl.BlockSpec((1, D), const2),
            pl.BlockSpec((MIX, D), const2),
            pl.BlockSpec((MIX, D), const2),
            pl.BlockSpec((D, D), const2),
        ],
        out_specs=tok(D),
        out_shape=jax.ShapeDtypeStruct((B, S, D), x.dtype),
        compiler_params=pltpu.CompilerParams(
            dimension_semantics=("parallel", "parallel"),
            vmem_limit_bytes=V7X_VMEM_BYTES * 3 // 4),
        name="outproj",
    )(oa, obt, sdg, gain_b, gma, gmb, x, mod3, g_post, wa, wb, wo)


def kernel(x, c, w_ada, b_ada, g_pre, g_post, w_in, lb_logits, hg_norm_gain,
           lambda_q1, lambda_k1, lambda_q2, lambda_k2, diff_norm_gain,
           w_branch_a, w_branch_b, w_out):
    B, S, D = x.shape
    assert w_in.shape[0] == 1 and lb_logits.shape[0] == 2, "single-layer block"
    assert w_in.shape[2] == 8 * MIX + 2 * D
    tm = min(512, S)
    assert S % tm == 0 and tm % CHUNK == 0

    mod, lbv, lam = _prep(c, w_ada[0], b_ada, lb_logits, lambda_q1, lambda_k1, lambda_q2, lambda_k2)
    mod3 = mod.reshape(B, 3, D)

    o_a, dqt, dk, dvt, sdg, gma, gmb = _inproj(
        x, mod3, g_pre, lbv, w_in[0].astype(BF16), hg_norm_gain, tm, tm // 2)
    o_b_t = _attn(lam[0, :1], dqt, dk, dvt)

    return _outproj(o_a, o_b_t, sdg, diff_norm_gain, gma, gmb, x, mod3, g_post,
                    w_branch_a[0].astype(BF16), w_branch_b[0].astype(BF16), w_out[0].astype(BF16), tm)
```

```python
import functools
import math

import numpy as np
import jax
import jax.numpy as jnp
from jax import lax
from jax.experimental import pallas as pl
from jax.experimental.pallas import tpu as pltpu

F32 = jnp.float32
BF16 = jnp.bfloat16
RMS_EPS = 1e-6
LOG2E = 1.4426950408889634

HEAD = 128
DIFF_HEAD_DIM = 64
N_HEADS = 4
MIX = N_HEADS * HEAD
CHUNK = 128
PROJ_SLICE = 512
LEVELS = (2, 4, 8, 16, 32, 64)
CODE_SUB = 1 + len(LEVELS)
CODE_DIAG = 2 + len(LEVELS)
LAM_INIT = 0.8 - 0.6 * math.exp(-0.3 * 0)
ONES_ROWS = 16

V7X_VMEM_BYTES = 64 * 1024 * 1024
NT_DIMS = (((1,), (1,)), ((), ()))


def _sigmoid(z):
    return jax.nn.sigmoid(z)


def _prep_kernel(c_ref, w_ref, b_ref, lbl_ref, lq1_ref, lk1_ref, lq2_ref, lk2_ref,
                 mod_ref, lb_ref, lam_ref):
    c = c_ref[...]
    sc = c * _sigmoid(c)
    mod_ref[...] = jnp.dot(sc, w_ref[...], precision=lax.Precision.HIGHEST,
                           preferred_element_type=F32) + b_ref[...]
    z = lbl_ref[...]
    e = jnp.exp(z - jnp.max(z, axis=0, keepdims=True))
    lb_ref[...] = e / jnp.sum(e, axis=0, keepdims=True)
    d1 = jnp.sum(lq1_ref[...] * lk1_ref[...], axis=-1, keepdims=True)
    d2 = jnp.sum(lq2_ref[...] * lk2_ref[...], axis=-1, keepdims=True)
    lam = jnp.exp(d1) - jnp.exp(d2) + LAM_INIT
    lam_ref[...] = jnp.broadcast_to(lam, lam_ref.shape)


def _prep(c, w_ada, b_ada, lb_logits, lq1, lk1, lq2, lk2):
    B, D = c.shape
    n_tiles = w_ada.shape[1] // D
    const = lambda j: (0, 0)
    return pl.pallas_call(
        _prep_kernel,
        grid=(n_tiles,),
        in_specs=[
            pl.BlockSpec((B, D), const),
            pl.BlockSpec((D, D), lambda j: (0, j)),
            pl.BlockSpec((1, D), lambda j: (0, j)),
            pl.BlockSpec(lb_logits.shape, const),
            pl.BlockSpec(lq1.shape, const),
            pl.BlockSpec(lk1.shape, const),
            pl.BlockSpec(lq2.shape, const),
            pl.BlockSpec(lk2.shape, const),
        ],
        out_specs=[
            pl.BlockSpec((B, D), lambda j: (0, j)),
            pl.BlockSpec(lb_logits.shape, const),
            pl.BlockSpec((1, HEAD), const),
        ],
        out_shape=[
            jax.ShapeDtypeStruct((B, n_tiles * D), F32),
            jax.ShapeDtypeStruct(lb_logits.shape, F32),
            jax.ShapeDtypeStruct((1, HEAD), F32),
        ],
        compiler_params=pltpu.CompilerParams(dimension_semantics=("arbitrary",)),
        name="prep",
    )(c, w_ada, b_ada, lb_logits, lq1, lk1, lq2, lk2)


def _inproj_kernel(x_ref, mod_ref, gpre_ref, lb_ref, w_ref, gain_ref, tril_ref, lvl_ref,
                   oa_ref, dqt_ref, dk_ref, dvt_ref, sdg_ref, gma_ref, gmb_ref,
                   hq_ref, kin_ref, logf_ref, hi_ref, sg_ref, state_ref):
    @pl.when(pl.program_id(1) == 0)
    def _():
        state_ref[...] = jnp.zeros_like(state_ref)

    x = x_ref[...]
    ms = jnp.mean(x * x, axis=-1, keepdims=True)
    xn = x * lax.rsqrt(ms + RMS_EPS) * gpre_ref[...]
    h = xn * (1.0 + mod_ref[1:2, :]) + mod_ref[0:1, :]
    hb = h.astype(BF16)

    def proj(c0, width):
        return jnp.dot(hb, w_ref[:, c0:c0 + width], preferred_element_type=F32)

    hq_ref[...] = proj(0 * MIX, MIX).astype(BF16)
    sig = _sigmoid(proj(1 * MIX, MIX))
    lb = lb_ref[0:1, :]
    oml = lb_ref[1:2, :]
    logf_ref[...] = jnp.log(lb + oml * sig)
    kin_ref[...] = (oml * (1.0 - sig)).astype(BF16)
    hi_ref[...] = proj(2 * MIX, MIX).astype(BF16)
    z = proj(3 * MIX, MIX)
    sg_ref[...] = (z * _sigmoid(z)).astype(BF16)
    tk = dvt_ref.shape[-1]
    d_model = gma_ref.shape[-1]

    def q_slice(c, z):
        dqt_ref[c:c + PROJ_SLICE, :] = (z * (DIFF_HEAD_DIM ** -0.5 * LOG2E)).T.astype(BF16)

    def k_slice(c, z):
        dk_ref[:, c:c + PROJ_SLICE] = z.astype(BF16)

    def v_slice(c, z):
        z_t = z.T.astype(BF16)
        for u in range(dvt_ref.shape[0]):
            dvt_ref[u, c:c + PROJ_SLICE, :] = z_t[:, u * tk:(u + 1) * tk]

    def g_slice(c, z):
        sdg_ref[:, c:c + PROJ_SLICE] = (z * _sigmoid(z)).astype(BF16)

    def ma_slice(c, z):
        gma_ref[:, c:c + PROJ_SLICE] = _sigmoid(z).astype(BF16)

    def mb_slice(c, z):
        gmb_ref[:, c:c + PROJ_SLICE] = _sigmoid(z).astype(BF16)

    groups = ((q_slice, MIX), (k_slice, MIX), (v_slice, MIX), (g_slice, MIX),
              (ma_slice, d_model), (mb_slice, d_model))
    pieces, col0 = [], 4 * MIX
    for fn, width in groups:
        for c in range(0, width, PROJ_SLICE):
            pieces.append((functools.partial(proj, col0 + c, PROJ_SLICE), functools.partial(fn, c)))
        col0 += width
    _hgrn2_tile(hq_ref, kin_ref, logf_ref, hi_ref, sg_ref, gain_ref, tril_ref, lvl_ref,
                oa_ref, state_ref, pieces)


def _inproj(x, mod3, g_pre, lbv, w_in_bf, gain_a, tm, tk):
    B, S, D = x.shape
    n_in = w_in_bf.shape[1]
    nt = S // tm
    ku = tm // tk
    tril = jnp.asarray(np.tril(np.ones((CHUNK, CHUNK), np.float32)), BF16)
    lvl = jnp.asarray(_level_codes())
    tok = lambda width: pl.BlockSpec((None, tm, width), lambda b, i: (b, i, 0))
    const2 = lambda b, i: (0, 0)
    sds = lambda width, dt: jax.ShapeDtypeStruct((B, S, width), dt)
    return pl.pallas_call(
        _inproj_kernel,
        grid=(B, nt),
        in_specs=[
            tok(D),
            pl.BlockSpec((None, 3, D), lambda b, i: (b, 0, 0)),
            pl.BlockSpec((1, D), const2),
            pl.BlockSpec(lbv.shape, const2),
            pl.BlockSpec((D, n_in), const2, pipeline_mode=pl.Buffered(1)),
            pl.BlockSpec((1, MIX), const2),
            pl.BlockSpec((CHUNK, CHUNK), const2),
            pl.BlockSpec((CHUNK, CHUNK), const2),
        ],
        out_specs=[
            tok(MIX),
            pl.BlockSpec((None, None, MIX, tm), lambda b, i: (b, i, 0, 0)),
            tok(MIX),
            pl.BlockSpec((None, ku, MIX, tk), lambda b, i: (b, i, 0, 0)),
            tok(MIX), tok(D), tok(D),
        ],
        out_shape=[
            sds(MIX, BF16),
            jax.ShapeDtypeStruct((B, nt, MIX, tm), BF16),
            sds(MIX, BF16),
            jax.ShapeDtypeStruct((B, nt * ku, MIX, tk), BF16),
            sds(MIX, BF16), sds(D, BF16), sds(D, BF16),
        ],
        scratch_shapes=[
            pltpu.VMEM((tm, MIX), BF16), pltpu.VMEM((tm, MIX), BF16), pltpu.VMEM((tm, MIX), F32),
            pltpu.VMEM((tm, MIX), BF16), pltpu.VMEM((tm, MIX), BF16),
            pltpu.VMEM((N_HEADS, HEAD, HEAD), F32),
        ],
        compiler_params=pltpu.CompilerParams(
            dimension_semantics=("arbitrary", "arbitrary"),
            vmem_limit_bytes=V7X_VMEM_BYTES * 7 // 8),
        name="inproj_hgrn2",
    )(x, mod3, g_pre, lbv, w_in_bf, gain_a, tril, lvl)


def _level_codes():
    t = np.arange(CHUNK)[:, None]
    s = np.arange(CHUNK)[None, :]
    code = np.zeros((CHUNK, CHUNK), np.int32)
    for i, m in reversed(list(enumerate(LEVELS))):
        code = np.where((t // (2 * m) == s // (2 * m)) & (s < t), 1 + i, code)
    code = np.where((t // 2 == s // 2) & (s == t - 1), CODE_SUB, code)
    code = np.where(s == t, CODE_DIAG, code)
    return code.astype(np.int32)


def _split3(x):
    h1 = x.astype(BF16)
    r1 = x - h1.astype(F32)
    h2 = r1.astype(BF16)
    h3 = (r1 - h2.astype(F32)).astype(BF16)
    return h1, h2, h3


def _block_row(a, block, row):
    if block < 8:
        pos = lax.broadcasted_iota(jnp.int32, a.shape, 0) % block
        out = a
        for r in range(block):
            if r != row:
                out = jnp.where(pos == r, pltpu.roll(a, (r - row) % CHUNK, axis=0), out)
        return out
    g = CHUNK // block
    a3 = a.reshape(g, block, HEAD)
    return jnp.broadcast_to(a3[:, row:row + 1, :], (g, block, HEAD)).reshape(CHUNK, HEAD)


def _hgrn2_tile(hq_ref, kin_ref, logf_ref, hi_ref, sg_ref, gain_ref, tril_ref, lvl_ref,
                out_ref, state_ref, interleaved=()):
    pending = list(interleaved)
    tril = tril_ref[...]
    lvl = lvl_ref[...]
    n_chunks = hq_ref.shape[0] // CHUNK
    instances = [(ci, h) for ci in range(n_chunks) for h in range(N_HEADS)]
    stride = max(1, len(instances) // max(1, len(pending)))

    lf_chunks, b_chunks = [], []
    for ci in range(n_chunks):
        lf_all = logf_ref[ci * CHUNK:(ci + 1) * CHUNK, :] * LOG2E
        l1, l2, l3 = _split3(lf_all)
        lf_chunks.append(lf_all)
        b_chunks.append(jnp.dot(tril, l3, preferred_element_type=F32)
                        + jnp.dot(tril, l2, preferred_element_type=F32)
                        + jnp.dot(tril, l1, preferred_element_type=F32))

    def operands(ci, h):
        rows = slice(ci * CHUNK, (ci + 1) * CHUNK)
        cols = slice(h * HEAD, (h + 1) * HEAD)
        b = b_chunks[ci][:, cols]
        qb = hq_ref[rows, cols]
        kb = kin_ref[rows, cols]
        vb = hi_ref[rows, cols]
        q = qb.astype(F32)
        k = kb.astype(F32)
        b_last = b[CHUNK - 1:CHUNK, :]
        pairs = [(qb, kb), ((q * jnp.exp2(lf_chunks[ci][:, cols])).astype(BF16), kb)]
        for m in LEVELS:
            e = jnp.exp2(b - _block_row(b, 2 * m, m - 1))
            pairs.append(((q * e).astype(BF16), (k * (1.0 / e)).astype(BF16)))
        return dict(rows=rows, cols=cols, h=h, vb=vb, pairs=pairs,
                    qbar=(q * jnp.exp2(b)).astype(BF16),
                    kbar=(k * jnp.exp2(b_last - b)).astype(BF16),
                    v_t=vb.astype(F32).T.astype(BF16),
                    state_decay=jnp.exp2(b_last))

    def score_dots(op):
        state_t = state_ref[op["h"]]
        op["o_inter"] = lax.dot_general(op["qbar"], state_t.astype(BF16), NT_DIMS,
                                        preferred_element_type=F32)
        state_ref[op["h"]] = state_t * op["state_decay"] + jnp.dot(op["v_t"], op["kbar"],
                                                                   preferred_element_type=F32)
        op["parts"] = [lax.dot_general(qm, km, NT_DIMS, preferred_element_type=F32)
                       for qm, km in op["pairs"]]

    def finish(op):
        codes = [CODE_DIAG, CODE_SUB] + [1 + i for i in range(len(LEVELS))]
        scores = jnp.where(lvl == codes[0], op["parts"][0], 0.0)
        for code, part in zip(codes[1:], op["parts"][1:]):
            scores = jnp.where(lvl == code, part, scores)
        o = op["o_inter"] + jnp.dot(scores.astype(BF16), op["vb"], preferred_element_type=F32)
        o = o * lax.rsqrt(jnp.mean(o * o, axis=-1, keepdims=True) + RMS_EPS) * gain_ref[:, op["cols"]]
        out_ref[op["rows"], op["cols"]] = (o * sg_ref[op["rows"], op["cols"]].astype(F32)).astype(BF16)

    previous = []
    for g in range(0, len(instances), stride):
        group = [operands(ci, h) for ci, h in instances[g:g + stride]]
        piece = pending.pop(0) if pending else None
        if piece is not None:
            big = piece[0]()
        for op in group:
            score_dots(op)
        if piece is not None:
            piece[1](big)
        for op in previous:
            finish(op)
        previous = group
    for op in previous:
        finish(op)
    for dot, epilogue in pending:
        epilogue(dot())


def _attn_kernel(lam_ref, qt_ref, k_ref, vt_ref, ot_ref,
                 m_ref, acc_ref, s0_ref, s1_ref, qzt_ref):
    nq, _, tq = qt_ref.shape
    tk = vt_ref.shape[-1]
    i = pl.program_id(2)
    ones_rows = jnp.ones((ONES_ROWS, tk), BF16)

    def first_scores(qi):
        qt = qt_ref[qi].astype(F32)
        row = lax.broadcasted_iota(jnp.int32, qt.shape, 0)
        qzt_ref[...] = jnp.concatenate([jnp.where(row < DIFF_HEAD_DIM, qt, 0.0),
                                        jnp.where(row >= DIFF_HEAD_DIM, qt, 0.0)], axis=1).astype(BF16)
        s0_ref[...] = jnp.dot(k_ref[0:tk, :], qzt_ref[...], preferred_element_type=F32)

    @pl.when(i == 0)
    def _():
        first_scores(0)

    m_ref[...] = jnp.full(m_ref.shape, -jnp.inf, F32)
    acc_ref[...] = jnp.zeros(acc_ref.shape, F32)

    def key_block(j):
        return k_ref[pl.ds(pl.multiple_of(j * tk, tk), tk), :]

    def scores(j, s_ref):
        s_ref[...] = jnp.dot(key_block(j), qzt_ref[...], preferred_element_type=F32)

    def accumulate(j, s, lanes=slice(None)):
        m_old = m_ref[:, lanes]
        m_new = jnp.maximum(m_old, jnp.max(s, axis=0, keepdims=True))
        alpha = jnp.exp2(m_old - m_new)
        p = jnp.exp2(s - m_new).astype(BF16)
        vt = jnp.concatenate([vt_ref[j], ones_rows], axis=0)
        acc_ref[:, lanes] = alpha * acc_ref[:, lanes] + jnp.dot(vt, p, preferred_element_type=F32)
        m_ref[:, lanes] = m_new

    def pair(jj):
        scores(2 * jj + 1, s1_ref)
        accumulate(2 * jj, s0_ref[...])
        scores(2 * jj + 2, s0_ref)
        accumulate(2 * jj + 1, s1_ref[...])

    def four_pairs(t, carry):
        for u in range(4):
            pair(4 * t + u)
        return carry

    def two_pairs(t, carry):
        pair(i // 4 * 4)
        pair(i // 4 * 4 + 1)
        return carry

    def one_pair(t, carry):
        pair(i - 1)
        return carry

    lax.fori_loop(0, i // 4, four_pairs, 0)
    lax.fori_loop(0, i % 4 // 2, two_pairs, 0)
    lax.fori_loop(0, i % 2, one_pair, 0)

    kb_last = key_block(2 * i + 1)
    upper = [slice(mi * tq + tk, (mi + 1) * tq) for mi in range(2)]
    for lanes in upper:
        s1_ref[:, lanes] = jnp.dot(kb_last, qzt_ref[:, lanes], preferred_element_type=F32)

    s = s0_ref[...]
    key = lax.broadcasted_iota(jnp.int32, s.shape, 0)
    col = lax.broadcasted_iota(jnp.int32, s.shape, 1)
    qry = jnp.where(col >= tq, col - tq, col)
    accumulate(2 * i, jnp.where(key <= qry, s, -jnp.inf))

    first_scores(jnp.minimum(i + 1, nq - 1))

    key = lax.broadcasted_iota(jnp.int32, (tk, tk), 0)
    qry = lax.broadcasted_iota(jnp.int32, (tk, tk), 1)
    for lanes in upper:
        accumulate(2 * i + 1, jnp.where(key <= qry, s1_ref[:, lanes], -jnp.inf), lanes)

    acc = acc_ref[...]
    o_t = acc[0:HEAD, :] * (1.0 / acc[HEAD:HEAD + 1, :])
    o = o_t[:, :tq] - lam_ref[0] * o_t[:, tq:]
    ot_ref[...] = (o * lax.rsqrt(jnp.mean(o * o, axis=0, keepdims=True) + RMS_EPS)).astype(BF16)


def _attn(lam, dqt, dk, dvt):
    B, S, _ = dk.shape
    nq, tq = dqt.shape[1], dqt.shape[3]
    nkb, tk = dvt.shape[1], dvt.shape[3]
    assert tq == 2 * tk, "the diagonal of a query block is covered by exactly two key sub-blocks"
    return pl.pallas_call(
        _attn_kernel,
        grid=(B, N_HEADS, nq),
        in_specs=[
            pl.BlockSpec(memory_space=pltpu.SMEM),
            pl.BlockSpec((None, nq, HEAD, tq), lambda b, h, i: (b, 0, h, 0)),
            pl.BlockSpec((None, S, HEAD), lambda b, h, i: (b, 0, h)),
            pl.BlockSpec((None, nkb, HEAD, tk), lambda b, h, i: (b, 0, h, 0)),
        ],
        out_specs=pl.BlockSpec((None, HEAD, tq), lambda b, h, i: (b, h, i)),
        out_shape=jax.ShapeDtypeStruct((B, MIX, S), BF16),
        scratch_shapes=[pltpu.VMEM((1, 2 * tq), F32),
                        pltpu.VMEM((HEAD + ONES_ROWS, 2 * tq), F32),
                        pltpu.VMEM((tk, 2 * tq), F32),
                        pltpu.VMEM((tk, 2 * tq), F32),
                        pltpu.VMEM((HEAD, 2 * tq), BF16)],
        compiler_params=pltpu.CompilerParams(
            dimension_semantics=("arbitrary", "arbitrary", "arbitrary"),
            vmem_limit_bytes=V7X_VMEM_BYTES * 3 // 4),
        name="attn",
    )(lam, dqt, dk, dvt)


def _outproj_kernel(oa_ref, obt_ref, sdg_ref, gb_ref, gma_ref, gmb_ref, x_ref, mod_ref, gpost_ref,
                    wa_ref, wb_ref, wo_ref, out_ref):
    ya = jnp.dot(oa_ref[...], wa_ref[...], preferred_element_type=F32)
    ob = obt_ref[...].astype(F32).T * (gb_ref[...] * (1.0 - LAM_INIT)) * sdg_ref[...].astype(F32)
    yb = jnp.dot(ob.astype(BF16), wb_ref[...], preferred_element_type=F32)
    y = gma_ref[...].astype(F32) * ya + gmb_ref[...].astype(F32) * yb
    y = jnp.dot(y.astype(BF16), wo_ref[...], preferred_element_type=F32)
    yn = y * lax.rsqrt(jnp.mean(y * y, axis=-1, keepdims=True) + RMS_EPS) * gpost_ref[...]
    out_ref[...] = x_ref[...] + mod_ref[2:3, :] * yn


def _outproj(oa, obt, sdg, gain_b, gma, gmb, x, mod3, g_post, wa, wb, wo, tm):
    B, S, D = x.shape
    tok = lambda width: pl.BlockSpec((None, tm, width), lambda b, i: (b, i, 0))
    const2 = lambda b, i: (0, 0)
    return pl.pallas_call(
        _outproj_kernel,
        grid=(B, S // tm),
        in_specs=[
            tok(MIX),
            pl.BlockSpec((None, MIX, tm), lambda b, i: (b, 0, i)),
            tok(MIX),
            pl.BlockSpec((1, MIX), const2),
            tok(D), tok(D), tok(D),
            pl.BlockSpec((None, 3, D), lambda b, i: (b, 0, 0)),
            pl.BlockSpec((1, D), const2),
            pl.BlockSpec((MIX, D), const2),
            pl.BlockSpec((MIX, D), const2),
            pl.BlockSpec((D, D), const2),
        ],
        out_specs=tok(D),
        out_shape=jax.ShapeDtypeStruct((B, S, D), x.dtype),
        compiler_params=pltpu.CompilerParams(
            dimension_semantics=("parallel", "parallel"),
            vmem_limit_bytes=V7X_VMEM_BYTES * 3 // 4),
        name="outproj",
    )(oa, obt, sdg, gain_b, gma, gmb, x, mod3, g_post, wa, wb, wo)


def kernel(x, c, w_ada, b_ada, g_pre, g_post, w_in, lb_logits, hg_norm_gain,
           lambda_q1, lambda_k1, lambda_q2, lambda_k2, diff_norm_gain,
           w_branch_a, w_branch_b, w_out):
    B, S, D = x.shape
    assert w_in.shape[0] == 1 and lb_logits.shape[0] == 2, "single-layer block"
    assert w_in.shape[2] == 8 * MIX + 2 * D
    tm = min(512, S)
    assert S % tm == 0 and tm % CHUNK == 0

    mod, lbv, lam = _prep(c, w_ada[0], b_ada, lb_logits, lambda_q1, lambda_k1, lambda_q2, lambda_k2)
    mod3 = mod.reshape(B, 3, D)

    o_a, dqt, dk, dvt, sdg, gma, gmb = _inproj(
        x, mod3, g_pre, lbv, w_in[0].astype(BF16), hg_norm_gain, tm, tm // 2)
    o_b_t = _attn(lam[0, :1], dqt, dk, dvt)

    return _outproj(o_a, o_b_t, sdg, diff_norm_gain, gma, gmb, x, mod3, g_post,
                    w_branch_a[0].astype(BF16), w_branch_b[0].astype(BF16), w_out[0].astype(BF16), tm)
```

```python
import functools
import math

import numpy as np
import jax
import jax.numpy as jnp
from jax import lax
from jax.experimental import pallas as pl
from jax.experimental.pallas import tpu as pltpu

F32 = jnp.float32
BF16 = jnp.bfloat16
RMS_EPS = 1e-6
LOG2E = 1.4426950408889634

HEAD = 128
DIFF_HEAD_DIM = 64
N_HEADS = 4
MIX = N_HEADS * HEAD
CHUNK = 128
PROJ_SLICE = 512
LEVELS = (2, 4, 8, 16, 32, 64)
CODE_SUB = 1 + len(LEVELS)
CODE_DIAG = 2 + len(LEVELS)
LAM_INIT = 0.8 - 0.6 * math.exp(-0.3 * 0)
ONES_ROWS = 16
ATTN_HEADS_PER_STEP = 2

V7X_VMEM_BYTES = 64 * 1024 * 1024
NT_DIMS = (((1,), (1,)), ((), ()))


def _sigmoid(z):
    return jax.nn.sigmoid(z)


def _prep_kernel(c_ref, w_ref, b_ref, lbl_ref, lq1_ref, lk1_ref, lq2_ref, lk2_ref,
                 mod_ref, lb_ref, lam_ref):
    c = c_ref[...]
    sc = c * _sigmoid(c)
    mod_ref[...] = jnp.dot(sc, w_ref[...], precision=lax.Precision.HIGHEST,
                           preferred_element_type=F32) + b_ref[...]
    z = lbl_ref[...]
    e = jnp.exp(z - jnp.max(z, axis=0, keepdims=True))
    lb_ref[...] = e / jnp.sum(e, axis=0, keepdims=True)
    d1 = jnp.sum(lq1_ref[...] * lk1_ref[...], axis=-1, keepdims=True)
    d2 = jnp.sum(lq2_ref[...] * lk2_ref[...], axis=-1, keepdims=True)
    lam = jnp.exp(d1) - jnp.exp(d2) + LAM_INIT
    lam_ref[...] = jnp.broadcast_to(lam, lam_ref.shape)


def _prep(c, w_ada, b_ada, lb_logits, lq1, lk1, lq2, lk2):
    B, D = c.shape
    n_tiles = w_ada.shape[1] // D
    const = lambda j: (0, 0)
    return pl.pallas_call(
        _prep_kernel,
        grid=(n_tiles,),
        in_specs=[
            pl.BlockSpec((B, D), const),
            pl.BlockSpec((D, D), lambda j: (0, j)),
            pl.BlockSpec((1, D), lambda j: (0, j)),
            pl.BlockSpec(lb_logits.shape, const),
            pl.BlockSpec(lq1.shape, const),
            pl.BlockSpec(lk1.shape, const),
            pl.BlockSpec(lq2.shape, const),
            pl.BlockSpec(lk2.shape, const),
        ],
        out_specs=[
            pl.BlockSpec((B, D), lambda j: (0, j)),
            pl.BlockSpec(lb_logits.shape, const),
            pl.BlockSpec((1, HEAD), const),
        ],
        out_shape=[
            jax.ShapeDtypeStruct((B, n_tiles * D), F32),
            jax.ShapeDtypeStruct(lb_logits.shape, F32),
            jax.ShapeDtypeStruct((1, HEAD), F32),
        ],
        compiler_params=pltpu.CompilerParams(dimension_semantics=("arbitrary",)),
        name="prep",
    )(c, w_ada, b_ada, lb_logits, lq1, lk1, lq2, lk2)


def _inproj_kernel(x_ref, mod_ref, gpre_ref, lb_ref, w_ref, gain_ref, tril_ref, lvl_ref,
                   oa_ref, dqt_ref, dk_ref, dvt_ref, sdg_ref, gma_ref, gmb_ref,
                   hq_ref, kin_ref, logf_ref, hi_ref, sg_ref, state_ref):
    @pl.when(pl.program_id(1) == 0)
    def _():
        state_ref[...] = jnp.zeros_like(state_ref)

    x = x_ref[...]
    ms = jnp.mean(x * x, axis=-1, keepdims=True)
    xn = x * lax.rsqrt(ms + RMS_EPS) * gpre_ref[...]
    h = xn * (1.0 + mod_ref[1:2, :]) + mod_ref[0:1, :]
    hb = h.astype(BF16)

    def proj(c0, width):
        return jnp.dot(hb, w_ref[:, c0:c0 + width], preferred_element_type=F32)

    hq_ref[...] = proj(0 * MIX, MIX).astype(BF16)
    sig = _sigmoid(proj(1 * MIX, MIX))
    lb = lb_ref[0:1, :]
    oml = lb_ref[1:2, :]
    logf_ref[...] = jnp.log(lb + oml * sig)
    kin_ref[...] = (oml * (1.0 - sig)).astype(BF16)
    hi_ref[...] = proj(2 * MIX, MIX).astype(BF16)
    z = proj(3 * MIX, MIX)
    sg_ref[...] = (z * _sigmoid(z)).astype(BF16)
    tk = dvt_ref.shape[-1]
    d_model = gma_ref.shape[-1]

    def q_slice(c, z):
        dqt_ref[c:c + PROJ_SLICE, :] = (z * (DIFF_HEAD_DIM ** -0.5 * LOG2E)).T.astype(BF16)

    def k_slice(c, z):
        dk_ref[:, c:c + PROJ_SLICE] = z.astype(BF16)

    def v_slice(c, z):
        z_t = z.T.astype(BF16)
        for u in range(dvt_ref.shape[0]):
            dvt_ref[u, c:c + PROJ_SLICE, :] = z_t[:, u * tk:(u + 1) * tk]

    def g_slice(c, z):
        sdg_ref[:, c:c + PROJ_SLICE] = (z * _sigmoid(z)).astype(BF16)

    def ma_slice(c, z):
        gma_ref[:, c:c + PROJ_SLICE] = _sigmoid(z).astype(BF16)

    def mb_slice(c, z):
        gmb_ref[:, c:c + PROJ_SLICE] = _sigmoid(z).astype(BF16)

    groups = ((q_slice, MIX), (k_slice, MIX), (v_slice, MIX), (g_slice, MIX),
              (ma_slice, d_model), (mb_slice, d_model))
    pieces, col0 = [], 4 * MIX
    for fn, width in groups:
        for c in range(0, width, PROJ_SLICE):
            pieces.append((functools.partial(proj, col0 + c, PROJ_SLICE), functools.partial(fn, c)))
        col0 += width
    _hgrn2_tile(hq_ref, kin_ref, logf_ref, hi_ref, sg_ref, gain_ref, tril_ref, lvl_ref,
                oa_ref, state_ref, pieces)


def _inproj(x, mod3, g_pre, lbv, w_in_bf, gain_a, tm, tk):
    B, S, D = x.shape
    n_in = w_in_bf.shape[1]
    nt = S // tm
    ku = tm // tk
    tril = jnp.asarray(np.tril(np.ones((CHUNK, CHUNK), np.float32)), BF16)
    lvl = jnp.asarray(_level_codes())
    tok = lambda width: pl.BlockSpec((None, tm, width), lambda b, i: (b, i, 0))
    const2 = lambda b, i: (0, 0)
    sds = lambda width, dt: jax.ShapeDtypeStruct((B, S, width), dt)
    return pl.pallas_call(
        _inproj_kernel,
        grid=(B, nt),
        in_specs=[
            tok(D),
            pl.BlockSpec((None, 3, D), lambda b, i: (b, 0, 0)),
            pl.BlockSpec((1, D), const2),
            pl.BlockSpec(lbv.shape, const2),
            pl.BlockSpec((D, n_in), const2, pipeline_mode=pl.Buffered(1)),
            pl.BlockSpec((1, MIX), const2),
            pl.BlockSpec((CHUNK, CHUNK), const2),
            pl.BlockSpec((CHUNK, CHUNK), const2),
        ],
        out_specs=[
            tok(MIX),
            pl.BlockSpec((None, None, MIX, tm), lambda b, i: (b, i, 0, 0)),
            tok(MIX),
            pl.BlockSpec((None, ku, MIX, tk), lambda b, i: (b, i, 0, 0)),
            tok(MIX), tok(D), tok(D),
        ],
        out_shape=[
            sds(MIX, BF16),
            jax.ShapeDtypeStruct((B, nt, MIX, tm), BF16),
            sds(MIX, BF16),
            jax.ShapeDtypeStruct((B, nt * ku, MIX, tk), BF16),
            sds(MIX, BF16), sds(D, BF16), sds(D, BF16),
        ],
        scratch_shapes=[
            pltpu.VMEM((tm, MIX), BF16), pltpu.VMEM((tm, MIX), BF16), pltpu.VMEM((tm, MIX), F32),
            pltpu.VMEM((tm, MIX), BF16), pltpu.VMEM((tm, MIX), BF16),
            pltpu.VMEM((N_HEADS, HEAD, HEAD), F32),
        ],
        compiler_params=pltpu.CompilerParams(
            dimension_semantics=("arbitrary", "arbitrary"),
            vmem_limit_bytes=V7X_VMEM_BYTES * 7 // 8),
        name="inproj_hgrn2",
    )(x, mod3, g_pre, lbv, w_in_bf, gain_a, tril, lvl)


def _level_codes():
    t = np.arange(CHUNK)[:, None]
    s = np.arange(CHUNK)[None, :]
    code = np.zeros((CHUNK, CHUNK), np.int32)
    for i, m in reversed(list(enumerate(LEVELS))):
        code = np.where((t // (2 * m) == s // (2 * m)) & (s < t), 1 + i, code)
    code = np.where((t // 2 == s // 2) & (s == t - 1), CODE_SUB, code)
    code = np.where(s == t, CODE_DIAG, code)
    return code.astype(np.int32)


def _split3(x):
    h1 = x.astype(BF16)
    r1 = x - h1.astype(F32)
    h2 = r1.astype(BF16)
    h3 = (r1 - h2.astype(F32)).astype(BF16)
    return h1, h2, h3


def _block_row(a, block, row):
    if block < 8:
        pos = lax.broadcasted_iota(jnp.int32, a.shape, 0) % block
        out = a
        for r in range(block):
            if r != row:
                out = jnp.where(pos == r, pltpu.roll(a, (r - row) % CHUNK, axis=0), out)
        return out
    g = CHUNK // block
    a3 = a.reshape(g, block, HEAD)
    return jnp.broadcast_to(a3[:, row:row + 1, :], (g, block, HEAD)).reshape(CHUNK, HEAD)


def _hgrn2_tile(hq_ref, kin_ref, logf_ref, hi_ref, sg_ref, gain_ref, tril_ref, lvl_ref,
                out_ref, state_ref, interleaved=()):
    pending = list(interleaved)
    tril = tril_ref[...]
    lvl = lvl_ref[...]
    n_chunks = hq_ref.shape[0] // CHUNK
    instances = [(ci, h) for ci in range(n_chunks) for h in range(N_HEADS)]
    stride = max(1, len(instances) // max(1, len(pending)))

    lf_chunks, b_chunks = [], []
    for ci in range(n_chunks):
        lf_all = logf_ref[ci * CHUNK:(ci + 1) * CHUNK, :] * LOG2E
        l1, l2, l3 = _split3(lf_all)
        lf_chunks.append(lf_all)
        b_chunks.append(jnp.dot(tril, l3, preferred_element_type=F32)
                        + jnp.dot(tril, l2, preferred_element_type=F32)
                        + jnp.dot(tril, l1, preferred_element_type=F32))

    def operands(ci, h):
        rows = slice(ci * CHUNK, (ci + 1) * CHUNK)
        cols = slice(h * HEAD, (h + 1) * HEAD)
        b = b_chunks[ci][:, cols]
        qb = hq_ref[rows, cols]
        kb = kin_ref[rows, cols]
        vb = hi_ref[rows, cols]
        q = qb.astype(F32)
        k = kb.astype(F32)
        b_last = b[CHUNK - 1:CHUNK, :]
        pairs = [(qb, kb), ((q * jnp.exp2(lf_chunks[ci][:, cols])).astype(BF16), kb)]
        for m in LEVELS:
            e = jnp.exp2(b - _block_row(b, 2 * m, m - 1))
            pairs.append(((q * e).astype(BF16), (k * (1.0 / e)).astype(BF16)))
        return dict(rows=rows, cols=cols, h=h, vb=vb, pairs=pairs,
                    qbar=(q * jnp.exp2(b)).astype(BF16),
                    kbar=(k * jnp.exp2(b_last - b)).astype(BF16),
                    v_t=vb.astype(F32).T.astype(BF16),
                    state_decay=jnp.exp2(b_last))

    def score_dots(op):
        state_t = state_ref[op["h"]]
        op["o_inter"] = lax.dot_general(op["qbar"], state_t.astype(BF16), NT_DIMS,
                                        preferred_element_type=F32)
        state_ref[op["h"]] = state_t * op["state_decay"] + jnp.dot(op["v_t"], op["kbar"],
                                                                   preferred_element_type=F32)
        op["parts"] = [lax.dot_general(qm, km, NT_DIMS, preferred_element_type=F32)
                       for qm, km in op["pairs"]]

    def finish(op):
        codes = [CODE_DIAG, CODE_SUB] + [1 + i for i in range(len(LEVELS))]
        scores = jnp.where(lvl == codes[0], op["parts"][0], 0.0)
        for code, part in zip(codes[1:], op["parts"][1:]):
            scores = jnp.where(lvl == code, part, scores)
        o = op["o_inter"] + jnp.dot(scores.astype(BF16), op["vb"], preferred_element_type=F32)
        o = o * lax.rsqrt(jnp.mean(o * o, axis=-1, keepdims=True) + RMS_EPS) * gain_ref[:, op["cols"]]
        out_ref[op["rows"], op["cols"]] = (o * sg_ref[op["rows"], op["cols"]].astype(F32)).astype(BF16)

    previous = []
    for g in range(0, len(instances), stride):
        group = [operands(ci, h) for ci, h in instances[g:g + stride]]
        piece = pending.pop(0) if pending else None
        if piece is not None:
            big = piece[0]()
        for op in group:
            score_dots(op)
        if piece is not None:
            piece[1](big)
        for op in previous:
            finish(op)
        previous = group
    for op in previous:
        finish(op)
    for dot, epilogue in pending:
        epilogue(dot())


def _attn_kernel(lam_ref, qt_ref, k_ref, vt_ref, ot_ref,
                 m_ref, acc_ref, s0_ref, s1_ref, qzt_ref):
    nq, _, tq = qt_ref.shape
    tk = vt_ref.shape[-1]
    i = pl.program_id(2)
    heads = range(ATTN_HEADS_PER_STEP)
    ones_rows = jnp.ones((ONES_ROWS, tk), BF16)

    def head_rows(hh):
        return slice(hh * HEAD, (hh + 1) * HEAD)

    def first_scores(qi, hh):
        qt = qt_ref[qi, head_rows(hh), :].astype(F32)
        row = lax.broadcasted_iota(jnp.int32, qt.shape, 0)
        qzt_ref[hh] = jnp.concatenate([jnp.where(row < DIFF_HEAD_DIM, qt, 0.0),
                                       jnp.where(row >= DIFF_HEAD_DIM, qt, 0.0)], axis=1).astype(BF16)
        s0_ref[hh] = jnp.dot(k_ref[0:tk, head_rows(hh)], qzt_ref[hh], preferred_element_type=F32)

    @pl.when(i == 0)
    def _():
        for hh in heads:
            first_scores(0, hh)

    m_ref[...] = jnp.full(m_ref.shape, -jnp.inf, F32)
    acc_ref[...] = jnp.zeros(acc_ref.shape, F32)

    def key_block(j, hh):
        return k_ref[pl.ds(pl.multiple_of(j * tk, tk), tk), head_rows(hh)]

    def scores(j, s_ref, hh):
        s_ref[hh] = jnp.dot(key_block(j, hh), qzt_ref[hh], preferred_element_type=F32)

    def accumulate(j, s, hh, lanes=slice(None)):
        m_old = m_ref[hh, :, lanes]
        m_new = jnp.maximum(m_old, jnp.max(s, axis=0, keepdims=True))
        alpha = jnp.exp2(m_old - m_new)
        p = jnp.exp2(s - m_new).astype(BF16)
        vt = jnp.concatenate([vt_ref[j, head_rows(hh), :], ones_rows], axis=0)
        acc_ref[hh, :, lanes] = alpha * acc_ref[hh, :, lanes] + jnp.dot(vt, p, preferred_element_type=F32)
        m_ref[hh, :, lanes] = m_new

    def pair(jj, hh):
        scores(2 * jj + 1, s1_ref, hh)
        accumulate(2 * jj, s0_ref[hh], hh)
        scores(2 * jj + 2, s0_ref, hh)
        accumulate(2 * jj + 1, s1_ref[hh], hh)

    def four_pairs(t, carry):
        for hh in heads:
            for u in range(4):
                pair(4 * t + u, hh)
        return carry

    def two_pairs(t, carry):
        for hh in heads:
            pair(i // 4 * 4, hh)
            pair(i // 4 * 4 + 1, hh)
        return carry

    def one_pair(t, carry):
        for hh in heads:
            pair(i - 1, hh)
        return carry

    lax.fori_loop(0, i // 4, four_pairs, 0)
    lax.fori_loop(0, i % 4 // 2, two_pairs, 0)
    lax.fori_loop(0, i % 2, one_pair, 0)

    upper = [slice(mi * tq + tk, (mi + 1) * tq) for mi in range(2)]
    for hh in heads:
        kb_last = key_block(2 * i + 1, hh)
        for lanes in upper:
            s1_ref[hh, :, lanes] = jnp.dot(kb_last, qzt_ref[hh, :, lanes], preferred_element_type=F32)

    key = lax.broadcasted_iota(jnp.int32, (tk, 2 * tq), 0)
    col = lax.broadcasted_iota(jnp.int32, (tk, 2 * tq), 1)
    qry = jnp.where(col >= tq, col - tq, col)
    for hh in heads:
        accumulate(2 * i, jnp.where(key <= qry, s0_ref[hh], -jnp.inf), hh)
        first_scores(jnp.minimum(i + 1, nq - 1), hh)

    key = lax.broadcasted_iota(jnp.int32, (tk, tk), 0)
    qry = lax.broadcasted_iota(jnp.int32, (tk, tk), 1)
    for hh in heads:
        for lanes in upper:
            accumulate(2 * i + 1, jnp.where(key <= qry, s1_ref[hh, :, lanes], -jnp.inf), hh, lanes)

    for hh in heads:
        acc = acc_ref[hh]
        o_t = acc[0:HEAD, :] * (1.0 / acc[HEAD:HEAD + 1, :])
        o = o_t[:, :tq] - lam_ref[0] * o_t[:, tq:]
        ot_ref[head_rows(hh), :] = (o * lax.rsqrt(jnp.mean(o * o, axis=0, keepdims=True) + RMS_EPS)
                                    ).astype(BF16)


def _attn(lam, dqt, dk, dvt):
    B, S, _ = dk.shape
    nq, tq = dqt.shape[1], dqt.shape[3]
    nkb, tk = dvt.shape[1], dvt.shape[3]
    assert tq == 2 * tk, "the diagonal of a query block is covered by exactly two key sub-blocks"
    hps = ATTN_HEADS_PER_STEP
    width = hps * HEAD
    return pl.pallas_call(
        _attn_kernel,
        grid=(B, N_HEADS // hps, nq),
        in_specs=[
            pl.BlockSpec(memory_space=pltpu.SMEM),
            pl.BlockSpec((None, nq, width, tq), lambda b, g, i: (b, 0, g, 0)),
            pl.BlockSpec((None, S, width), lambda b, g, i: (b, 0, g)),
            pl.BlockSpec((None, nkb, width, tk), lambda b, g, i: (b, 0, g, 0)),
        ],
        out_specs=pl.BlockSpec((None, width, tq), lambda b, g, i: (b, g, i)),
        out_shape=jax.ShapeDtypeStruct((B, MIX, S), BF16),
        scratch_shapes=[pltpu.VMEM((hps, 1, 2 * tq), F32),
                        pltpu.VMEM((hps, HEAD + ONES_ROWS, 2 * tq), F32),
                        pltpu.VMEM((hps, tk, 2 * tq), F32),
                        pltpu.VMEM((hps, tk, 2 * tq), F32),
                        pltpu.VMEM((hps, HEAD, 2 * tq), BF16)],
        compiler_params=pltpu.CompilerParams(
            dimension_semantics=("arbitrary", "arbitrary", "arbitrary"),
            vmem_limit_bytes=V7X_VMEM_BYTES * 3 // 4),
        name="attn",
    )(lam, dqt, dk, dvt)


def _outproj_kernel(oa_ref, obt_ref, sdg_ref, gb_ref, gma_ref, gmb_ref, x_ref, mod_ref, gpost_ref,
                    wa_ref, wb_ref, wo_ref, out_ref):
    ya = jnp.dot(oa_ref[...], wa_ref[...], preferred_element_type=F32)
    ob = obt_ref[...].astype(F32).T * (gb_ref[...] * (1.0 - LAM_INIT)) * sdg_ref[...].astype(F32)
    yb = jnp.dot(ob.astype(BF16), wb_ref[...], preferred_element_type=F32)
    y = gma_ref[...].astype(F32) * ya + gmb_ref[...].astype(F32) * yb
    y = jnp.dot(y.astype(BF16), wo_ref[...], preferred_element_type=F32)
    yn = y * lax.rsqrt(jnp.mean(y * y, axis=-1, keepdims=True) + RMS_EPS) * gpost_ref[...]
    out_ref[...] = x_ref[...] + mod_ref[2:3, :] * yn


def _outproj(oa, obt, sdg, gain_b, gma, gmb, x, mod3, g_post, wa, wb, wo, tm):
    B, S, D = x.shape
    tok = lambda width: pl.BlockSpec((None, tm, width), lambda b, i: (b, i, 0))
    const2 = lambda b, i: (0, 0)
    return pl.pallas_call(
        _outproj_kernel,
        grid=(B, S // tm),
        in_specs=[
            tok(MIX),
            pl.BlockSpec((None, MIX, tm), lambda b, i: (b, 0, i)),
            tok(MIX),
            pl.BlockSpec((1, MIX), const2),
            tok(D), tok(D), tok(D),
            pl.BlockSpec((None, 3, D), lambda b, i: (b, 0, 0)),
            pl.BlockSpec((1, D), const2),
            pl.BlockSpec((MIX, D), const2),
            pl.BlockSpec((MIX, D), const2),
            pl.BlockSpec((D, D), const2),
        ],
        out_specs=tok(D),
        out_shape=jax.ShapeDtypeStruct((B, S, D), x.dtype),
        compiler_params=pltpu.CompilerParams(
            dimension_semantics=("parallel", "parallel"),
            vmem_limit_bytes=V7X_VMEM_BYTES * 3 // 4),
        name="outproj",
    )(oa, obt, sdg, gain_b, gma, gmb, x, mod3, g_post, wa, wb, wo)


def kernel(x, c, w_ada, b_ada, g_pre, g_post, w_in, lb_logits, hg_norm_gain,
           lambda_q1, lambda_k1, lambda_q2, lambda_k2, diff_norm_gain,
           w_branch_a, w_branch_b, w_out):
    B, S, D = x.shape
    assert w_in.shape[0] == 1 and lb_logits.shape[0] == 2, "single-layer block"
    assert w_in.shape[2] == 8 * MIX + 2 * D
    tm = min(512, S)
    assert S % tm == 0 and tm % CHUNK == 0

    mod, lbv, lam = _prep(c, w_ada[0], b_ada, lb_logits, lambda_q1, lambda_k1, lambda_q2, lambda_k2)
    mod3 = mod.reshape(B, 3, D)

    o_a, dqt, dk, dvt, sdg, gma, gmb = _inproj(
        x, mod3, g_pre, lbv, w_in[0].astype(BF16), hg_norm_gain, tm, tm // 2)
    o_b_t = _attn(lam[0, :1], dqt, dk, dvt)

    return _outproj(o_a, o_b_t, sdg, diff_norm_gain, gma, gmb, x, mod3, g_post,
                    w_branch_a[0].astype(BF16), w_branch_b[0].astype(BF16), w_out[0].astype(BF16),
                    min(2 * tm, S))
```

```python
import functools
import math

import numpy as np
import jax
import jax.numpy as jnp
from jax import lax
from jax.experimental import pallas as pl
from jax.experimental.pallas import tpu as pltpu

F32 = jnp.float32
BF16 = jnp.bfloat16
RMS_EPS = 1e-6
LOG2E = 1.4426950408889634

HEAD = 128
DIFF_HEAD_DIM = 64
N_HEADS = 4
MIX = N_HEADS * HEAD
CHUNK = 128
PROJ_SLICE = 512
LEVELS = (2, 4, 8, 16, 32, 64)
CODE_SUB = 1 + len(LEVELS)
CODE_DIAG = 2 + len(LEVELS)
LAM_INIT = 0.8 - 0.6 * math.exp(-0.3 * 0)
ONES_ROWS = 16
ATTN_HEADS_PER_STEP = 2
OUT_ROW_CHUNK = 256

V7X_VMEM_BYTES = 64 * 1024 * 1024
NT_DIMS = (((1,), (1,)), ((), ()))


def _sigmoid(z):
    return jax.nn.sigmoid(z)


def _prep_kernel(c_ref, w_ref, b_ref, lbl_ref, lq1_ref, lk1_ref, lq2_ref, lk2_ref,
                 mod_ref, lb_ref, lam_ref):
    c = c_ref[...]
    sc = c * _sigmoid(c)
    mod_ref[...] = jnp.dot(sc, w_ref[...], precision=lax.Precision.HIGHEST,
                           preferred_element_type=F32) + b_ref[...]
    z = lbl_ref[...]
    e = jnp.exp(z - jnp.max(z, axis=0, keepdims=True))
    lb_ref[...] = e / jnp.sum(e, axis=0, keepdims=True)
    d1 = jnp.sum(lq1_ref[...] * lk1_ref[...], axis=-1, keepdims=True)
    d2 = jnp.sum(lq2_ref[...] * lk2_ref[...], axis=-1, keepdims=True)
    lam = jnp.exp(d1) - jnp.exp(d2) + LAM_INIT
    lam_ref[...] = jnp.broadcast_to(lam, lam_ref.shape)


def _prep(c, w_ada, b_ada, lb_logits, lq1, lk1, lq2, lk2):
    B, D = c.shape
    n_tiles = w_ada.shape[1] // D
    const = lambda j: (0, 0)
    return pl.pallas_call(
        _prep_kernel,
        grid=(n_tiles,),
        in_specs=[
            pl.BlockSpec((B, D), const),
            pl.BlockSpec((D, D), lambda j: (0, j)),
            pl.BlockSpec((1, D), lambda j: (0, j)),
            pl.BlockSpec(lb_logits.shape, const),
            pl.BlockSpec(lq1.shape, const),
            pl.BlockSpec(lk1.shape, const),
            pl.BlockSpec(lq2.shape, const),
            pl.BlockSpec(lk2.shape, const),
        ],
        out_specs=[
            pl.BlockSpec((B, D), lambda j: (0, j)),
            pl.BlockSpec(lb_logits.shape, const),
            pl.BlockSpec((1, HEAD), const),
        ],
        out_shape=[
            jax.ShapeDtypeStruct((B, n_tiles * D), F32),
            jax.ShapeDtypeStruct(lb_logits.shape, F32),
            jax.ShapeDtypeStruct((1, HEAD), F32),
        ],
        compiler_params=pltpu.CompilerParams(dimension_semantics=("arbitrary",)),
        name="prep",
    )(c, w_ada, b_ada, lb_logits, lq1, lk1, lq2, lk2)


def _inproj_kernel(x_ref, mod_ref, gpre_ref, lb_ref, w_ref, gain_ref, tril_ref, lvl_ref,
                   oa_ref, dqt_ref, dk_ref, dvt_ref, sdg_ref, gma_ref, gmb_ref,
                   hq_ref, kin_ref, logf_ref, hi_ref, sg_ref, state_ref):
    @pl.when(pl.program_id(1) == 0)
    def _():
        state_ref[...] = jnp.zeros_like(state_ref)

    x = x_ref[...]
    ms = jnp.mean(x * x, axis=-1, keepdims=True)
    xn = x * lax.rsqrt(ms + RMS_EPS) * gpre_ref[...]
    h = xn * (1.0 + mod_ref[1:2, :]) + mod_ref[0:1, :]
    hb = h.astype(BF16)

    def proj(c0, width):
        return jnp.dot(hb, w_ref[:, c0:c0 + width], preferred_element_type=F32)

    hq_ref[...] = proj(0 * MIX, MIX).astype(BF16)
    sig = _sigmoid(proj(1 * MIX, MIX))
    lb = lb_ref[0:1, :]
    oml = lb_ref[1:2, :]
    logf_ref[...] = jnp.log(lb + oml * sig)
    kin_ref[...] = (oml * (1.0 - sig)).astype(BF16)
    hi_ref[...] = proj(2 * MIX, MIX).astype(BF16)
    z = proj(3 * MIX, MIX)
    sg_ref[...] = (z * _sigmoid(z)).astype(BF16)
    tk = dvt_ref.shape[-1]
    d_model = gma_ref.shape[-1]

    def q_slice(c, z):
        dqt_ref[c:c + PROJ_SLICE, :] = (z * (DIFF_HEAD_DIM ** -0.5 * LOG2E)).T.astype(BF16)

    def k_slice(c, z):
        dk_ref[:, c:c + PROJ_SLICE] = z.astype(BF16)

    def v_slice(c, z):
        z_t = z.T.astype(BF16)
        for u in range(dvt_ref.shape[0]):
            dvt_ref[u, c:c + PROJ_SLICE, :] = z_t[:, u * tk:(u + 1) * tk]

    def g_slice(c, z):
        sdg_ref[:, c:c + PROJ_SLICE] = (z * _sigmoid(z)).astype(BF16)

    def ma_slice(c, z):
        gma_ref[:, c:c + PROJ_SLICE] = _sigmoid(z).astype(BF16)

    def mb_slice(c, z):
        gmb_ref[:, c:c + PROJ_SLICE] = _sigmoid(z).astype(BF16)

    groups = ((q_slice, MIX), (k_slice, MIX), (v_slice, MIX), (g_slice, MIX),
              (ma_slice, d_model), (mb_slice, d_model))
    pieces, col0 = [], 4 * MIX
    for fn, width in groups:
        for c in range(0, width, PROJ_SLICE):
            pieces.append((functools.partial(proj, col0 + c, PROJ_SLICE), functools.partial(fn, c)))
        col0 += width
    _hgrn2_tile(hq_ref, kin_ref, logf_ref, hi_ref, sg_ref, gain_ref, tril_ref, lvl_ref,
                oa_ref, state_ref, pieces)


def _inproj(x, mod3, g_pre, lbv, w_in_bf, gain_a, tm, tk):
    B, S, D = x.shape
    n_in = w_in_bf.shape[1]
    nt = S // tm
    ku = tm // tk
    tril = jnp.asarray(np.tril(np.ones((CHUNK, CHUNK), np.float32)), BF16)
    lvl = jnp.asarray(_level_codes())
    tok = lambda width: pl.BlockSpec((None, tm, width), lambda b, i: (b, i, 0))
    const2 = lambda b, i: (0, 0)
    sds = lambda width, dt: jax.ShapeDtypeStruct((B, S, width), dt)
    return pl.pallas_call(
        _inproj_kernel,
        grid=(B, nt),
        in_specs=[
            tok(D),
            pl.BlockSpec((None, 3, D), lambda b, i: (b, 0, 0)),
            pl.BlockSpec((1, D), const2),
            pl.BlockSpec(lbv.shape, const2),
            pl.BlockSpec((D, n_in), const2, pipeline_mode=pl.Buffered(1)),
            pl.BlockSpec((1, MIX), const2),
            pl.BlockSpec((CHUNK, CHUNK), const2),
            pl.BlockSpec((CHUNK, CHUNK), const2),
        ],
        out_specs=[
            tok(MIX),
            pl.BlockSpec((None, None, MIX, tm), lambda b, i: (b, i, 0, 0)),
            tok(MIX),
            pl.BlockSpec((None, ku, MIX, tk), lambda b, i: (b, i, 0, 0)),
            tok(MIX), tok(D), tok(D),
        ],
        out_shape=[
            sds(MIX, BF16),
            jax.ShapeDtypeStruct((B, nt, MIX, tm), BF16),
            sds(MIX, BF16),
            jax.ShapeDtypeStruct((B, nt * ku, MIX, tk), BF16),
            sds(MIX, BF16), sds(D, BF16), sds(D, BF16),
        ],
        scratch_shapes=[
            pltpu.VMEM((tm, MIX), BF16), pltpu.VMEM((tm, MIX), BF16), pltpu.VMEM((tm, MIX), F32),
            pltpu.VMEM((tm, MIX), BF16), pltpu.VMEM((tm, MIX), BF16),
            pltpu.VMEM((N_HEADS, HEAD, HEAD), F32),
        ],
        compiler_params=pltpu.CompilerParams(
            dimension_semantics=("arbitrary", "arbitrary"),
            vmem_limit_bytes=V7X_VMEM_BYTES * 7 // 8),
        name="inproj_hgrn2",
    )(x, mod3, g_pre, lbv, w_in_bf, gain_a, tril, lvl)


def _level_codes():
    t = np.arange(CHUNK)[:, None]
    s = np.arange(CHUNK)[None, :]
    code = np.zeros((CHUNK, CHUNK), np.int32)
    for i, m in reversed(list(enumerate(LEVELS))):
        code = np.where((t // (2 * m) == s // (2 * m)) & (s < t), 1 + i, code)
    code = np.where((t // 2 == s // 2) & (s == t - 1), CODE_SUB, code)
    code = np.where(s == t, CODE_DIAG, code)
    return code.astype(np.int32)


def _split3(x):
    h1 = x.astype(BF16)
    r1 = x - h1.astype(F32)
    h2 = r1.astype(BF16)
    h3 = (r1 - h2.astype(F32)).astype(BF16)
    return h1, h2, h3


def _block_row(a, block, row):
    if block < 8:
        pos = lax.broadcasted_iota(jnp.int32, a.shape, 0) % block
        out = a
        for r in range(block):
            if r != row:
                out = jnp.where(pos == r, pltpu.roll(a, (r - row) % CHUNK, axis=0), out)
        return out
    g = CHUNK // block
    a3 = a.reshape(g, block, HEAD)
    return jnp.broadcast_to(a3[:, row:row + 1, :], (g, block, HEAD)).reshape(CHUNK, HEAD)


def _hgrn2_tile(hq_ref, kin_ref, logf_ref, hi_ref, sg_ref, gain_ref, tril_ref, lvl_ref,
                out_ref, state_ref, interleaved=()):
    pending = list(interleaved)
    tril = tril_ref[...]
    lvl = lvl_ref[...]
    n_chunks = hq_ref.shape[0] // CHUNK
    instances = [(ci, h) for ci in range(n_chunks) for h in range(N_HEADS)]
    stride = max(1, len(instances) // max(1, len(pending)))

    lf_chunks, b_chunks = [], []
    for ci in range(n_chunks):
        lf_all = logf_ref[ci * CHUNK:(ci + 1) * CHUNK, :] * LOG2E
        l1, l2, l3 = _split3(lf_all)
        lf_chunks.append(lf_all)
        b_chunks.append(jnp.dot(tril, l3, preferred_element_type=F32)
                        + jnp.dot(tril, l2, preferred_element_type=F32)
                        + jnp.dot(tril, l1, preferred_element_type=F32))

    def operands(ci, h):
        rows = slice(ci * CHUNK, (ci + 1) * CHUNK)
        cols = slice(h * HEAD, (h + 1) * HEAD)
        b = b_chunks[ci][:, cols]
        qb = hq_ref[rows, cols]
        kb = kin_ref[rows, cols]
        vb = hi_ref[rows, cols]
        q = qb.astype(F32)
        k = kb.astype(F32)
        b_last = b[CHUNK - 1:CHUNK, :]
        pairs = [(qb, kb), ((q * jnp.exp2(lf_chunks[ci][:, cols])).astype(BF16), kb)]
        for m in LEVELS:
            e = jnp.exp2(b - _block_row(b, 2 * m, m - 1))
            pairs.append(((q * e).astype(BF16), (k * (1.0 / e)).astype(BF16)))
        return dict(rows=rows, cols=cols, h=h, vb=vb, pairs=pairs,
                    qbar=(q * jnp.exp2(b)).astype(BF16),
                    kbar=(k * jnp.exp2(b_last - b)).astype(BF16),
                    v_t=vb.astype(F32).T.astype(BF16),
                    state_decay=jnp.exp2(b_last))

    def score_dots(op):
        state_t = state_ref[op["h"]]
        op["o_inter"] = lax.dot_general(op["qbar"], state_t.astype(BF16), NT_DIMS,
                                        preferred_element_type=F32)
        state_ref[op["h"]] = state_t * op["state_decay"] + jnp.dot(op["v_t"], op["kbar"],
                                                                   preferred_element_type=F32)
        op["parts"] = [lax.dot_general(qm, km, NT_DIMS, preferred_element_type=F32)
                       for qm, km in op["pairs"]]

    def finish(op):
        codes = [CODE_DIAG, CODE_SUB] + [1 + i for i in range(len(LEVELS))]
        scores = jnp.where(lvl == codes[0], op["parts"][0], 0.0)
        for code, part in zip(codes[1:], op["parts"][1:]):
            scores = jnp.where(lvl == code, part, scores)
        o = op["o_inter"] + jnp.dot(scores.astype(BF16), op["vb"], preferred_element_type=F32)
        o = o * lax.rsqrt(jnp.mean(o * o, axis=-1, keepdims=True) + RMS_EPS) * gain_ref[:, op["cols"]]
        out_ref[op["rows"], op["cols"]] = (o * sg_ref[op["rows"], op["cols"]].astype(F32)).astype(BF16)

    previous = []
    for g in range(0, len(instances), stride):
        group = [operands(ci, h) for ci, h in instances[g:g + stride]]
        piece = pending.pop(0) if pending else None
        if piece is not None:
            big = piece[0]()
        for op in group:
            score_dots(op)
        if piece is not None:
            piece[1](big)
        for op in previous:
            finish(op)
        previous = group
    for op in previous:
        finish(op)
    for dot, epilogue in pending:
        epilogue(dot())


def _attn_kernel(lam_ref, qt_ref, k_ref, vt_ref, ot_ref,
                 m_ref, acc_ref, s0_ref, s1_ref, qzt_ref):
    nq, _, tq = qt_ref.shape
    tk = vt_ref.shape[-1]
    i = pl.program_id(2)
    heads = range(ATTN_HEADS_PER_STEP)
    ones_rows = jnp.ones((ONES_ROWS, tk), BF16)

    def head_rows(hh):
        return slice(hh * HEAD, (hh + 1) * HEAD)

    def first_scores(qi, hh):
        qt = qt_ref[qi, head_rows(hh), :].astype(F32)
        row = lax.broadcasted_iota(jnp.int32, qt.shape, 0)
        qzt_ref[hh] = jnp.concatenate([jnp.where(row < DIFF_HEAD_DIM, qt, 0.0),
                                       jnp.where(row >= DIFF_HEAD_DIM, qt, 0.0)], axis=1).astype(BF16)
        s0_ref[hh] = jnp.dot(k_ref[0:tk, head_rows(hh)], qzt_ref[hh], preferred_element_type=F32)

    @pl.when(i == 0)
    def _():
        for hh in heads:
            first_scores(0, hh)

    m_ref[...] = jnp.full(m_ref.shape, -jnp.inf, F32)
    acc_ref[...] = jnp.zeros(acc_ref.shape, F32)

    def key_block(j, hh):
        return k_ref[pl.ds(pl.multiple_of(j * tk, tk), tk), head_rows(hh)]

    def scores(j, s_ref, hh):
        s_ref[hh] = jnp.dot(key_block(j, hh), qzt_ref[hh], preferred_element_type=F32)

    def accumulate(j, s, hh, lanes=slice(None)):
        m_old = m_ref[hh, :, lanes]
        m_new = jnp.maximum(m_old, jnp.max(s, axis=0, keepdims=True))
        alpha = jnp.exp2(m_old - m_new)
        p = jnp.exp2(s - m_new).astype(BF16)
        vt = jnp.concatenate([vt_ref[j, head_rows(hh), :], ones_rows], axis=0)
        acc_ref[hh, :, lanes] = alpha * acc_ref[hh, :, lanes] + jnp.dot(vt, p, preferred_element_type=F32)
        m_ref[hh, :, lanes] = m_new

    def pair(jj, hh):
        scores(2 * jj + 1, s1_ref, hh)
        accumulate(2 * jj, s0_ref[hh], hh)
        scores(2 * jj + 2, s0_ref, hh)
        accumulate(2 * jj + 1, s1_ref[hh], hh)

    def four_pairs(t, carry):
        for hh in heads:
            for u in range(4):
                pair(4 * t + u, hh)
        return carry

    def two_pairs(t, carry):
        for hh in heads:
            pair(i // 4 * 4, hh)
            pair(i // 4 * 4 + 1, hh)
        return carry

    def one_pair(t, carry):
        for hh in heads:
            pair(i - 1, hh)
        return carry

    lax.fori_loop(0, i // 4, four_pairs, 0)
    lax.fori_loop(0, i % 4 // 2, two_pairs, 0)
    lax.fori_loop(0, i % 2, one_pair, 0)

    upper = [slice(mi * tq + tk, (mi + 1) * tq) for mi in range(2)]
    for hh in heads:
        kb_last = key_block(2 * i + 1, hh)
        for lanes in upper:
            s1_ref[hh, :, lanes] = jnp.dot(kb_last, qzt_ref[hh, :, lanes], preferred_element_type=F32)

    causal = (lax.broadcasted_iota(jnp.int32, (tk, tk), 0)
              <= lax.broadcasted_iota(jnp.int32, (tk, tk), 1))
    for hh in heads:
        s = s0_ref[hh]
        s = jnp.concatenate([jnp.where(causal, s[:, 0:tk], -jnp.inf), s[:, tk:tq],
                             jnp.where(causal, s[:, tq:tq + tk], -jnp.inf), s[:, tq + tk:]], axis=1)
        accumulate(2 * i, s, hh)
        first_scores(jnp.minimum(i + 1, nq - 1), hh)

    for hh in heads:
        for lanes in upper:
            accumulate(2 * i + 1, jnp.where(causal, s1_ref[hh, :, lanes], -jnp.inf), hh, lanes)

    for hh in heads:
        acc = acc_ref[hh]
        o_t = acc[0:HEAD, :] * (1.0 / acc[HEAD:HEAD + 1, :])
        o = o_t[:, :tq] - lam_ref[0] * o_t[:, tq:]
        ot_ref[head_rows(hh), :] = (o * lax.rsqrt(jnp.mean(o * o, axis=0, keepdims=True) + RMS_EPS)
                                    ).astype(BF16)


def _attn(lam, dqt, dk, dvt):
    B, S, _ = dk.shape
    nq, tq = dqt.shape[1], dqt.shape[3]
    nkb, tk = dvt.shape[1], dvt.shape[3]
    assert tq == 2 * tk, "the diagonal of a query block is covered by exactly two key sub-blocks"
    hps = ATTN_HEADS_PER_STEP
    width = hps * HEAD
    return pl.pallas_call(
        _attn_kernel,
        grid=(B, N_HEADS // hps, nq),
        in_specs=[
            pl.BlockSpec(memory_space=pltpu.SMEM),
            pl.BlockSpec((None, nq, width, tq), lambda b, g, i: (b, 0, g, 0)),
            pl.BlockSpec((None, S, width), lambda b, g, i: (b, 0, g)),
            pl.BlockSpec((None, nkb, width, tk), lambda b, g, i: (b, 0, g, 0)),
        ],
        out_specs=pl.BlockSpec((None, width, tq), lambda b, g, i: (b, g, i)),
        out_shape=jax.ShapeDtypeStruct((B, MIX, S), BF16),
        scratch_shapes=[pltpu.VMEM((hps, 1, 2 * tq), F32),
                        pltpu.VMEM((hps, HEAD + ONES_ROWS, 2 * tq), F32),
                        pltpu.VMEM((hps, tk, 2 * tq), F32),
                        pltpu.VMEM((hps, tk, 2 * tq), F32),
                        pltpu.VMEM((hps, HEAD, 2 * tq), BF16)],
        compiler_params=pltpu.CompilerParams(
            dimension_semantics=("arbitrary", "arbitrary", "arbitrary"),
            vmem_limit_bytes=V7X_VMEM_BYTES * 3 // 4),
        name="attn",
    )(lam, dqt, dk, dvt)


def _outproj_kernel(oa_ref, obt_ref, sdg_ref, gb_ref, gma_ref, gmb_ref, x_ref, mod_ref, gpost_ref,
                    wa_ref, wb_ref, wo_ref, out_ref):
    n_chunks = out_ref.shape[0] // OUT_ROW_CHUNK

    def matmuls(r):
        rows = slice(r * OUT_ROW_CHUNK, (r + 1) * OUT_ROW_CHUNK)
        ya = jnp.dot(oa_ref[rows, :], wa_ref[...], preferred_element_type=F32)
        ob = (obt_ref[:, rows].astype(F32).T * (gb_ref[...] * (1.0 - LAM_INIT))
              * sdg_ref[rows, :].astype(F32))
        yb = jnp.dot(ob.astype(BF16), wb_ref[...], preferred_element_type=F32)
        y = gma_ref[rows, :].astype(F32) * ya + gmb_ref[rows, :].astype(F32) * yb
        return jnp.dot(y.astype(BF16), wo_ref[...], preferred_element_type=F32)

    def epilogue(r, y):
        rows = slice(r * OUT_ROW_CHUNK, (r + 1) * OUT_ROW_CHUNK)
        yn = y * lax.rsqrt(jnp.mean(y * y, axis=-1, keepdims=True) + RMS_EPS) * gpost_ref[...]
        out_ref[rows, :] = x_ref[rows, :] + mod_ref[2:3, :] * yn

    previous = None
    for r in range(n_chunks):
        y = matmuls(r)
        if previous is not None:
            epilogue(*previous)
        previous = (r, y)
    epilogue(*previous)


def _outproj(oa, obt, sdg, gain_b, gma, gmb, x, mod3, g_post, wa, wb, wo, tm):
    B, S, D = x.shape
    assert S % tm == 0 and tm % OUT_ROW_CHUNK == 0
    tok = lambda width: pl.BlockSpec((None, tm, width), lambda b, i: (b, i, 0))
    const2 = lambda b, i: (0, 0)
    return pl.pallas_call(
        _outproj_kernel,
        grid=(B, S // tm),
        in_specs=[
            tok(MIX),
            pl.BlockSpec((None, MIX, tm), lambda b, i: (b, 0, i)),
            tok(MIX),
            pl.BlockSpec((1, MIX), const2),
            tok(D), tok(D), tok(D),
            pl.BlockSpec((None, 3, D), lambda b, i: (b, 0, 0)),
            pl.BlockSpec((1, D), const2),
            pl.BlockSpec((MIX, D), const2),
            pl.BlockSpec((MIX, D), const2),
            pl.BlockSpec((D, D), const2),
        ],
        out_specs=tok(D),
        out_shape=jax.ShapeDtypeStruct((B, S, D), x.dtype),
        compiler_params=pltpu.CompilerParams(
            dimension_semantics=("parallel", "parallel"),
            vmem_limit_bytes=V7X_VMEM_BYTES * 3 // 4),
        name="outproj",
    )(oa, obt, sdg, gain_b, gma, gmb, x, mod3, g_post, wa, wb, wo)


def kernel(x, c, w_ada, b_ada, g_pre, g_post, w_in, lb_logits, hg_norm_gain,
           lambda_q1, lambda_k1, lambda_q2, lambda_k2, diff_norm_gain,
           w_branch_a, w_branch_b, w_out):
    B, S, D = x.shape
    assert w_in.shape[0] == 1 and lb_logits.shape[0] == 2, "single-layer block"
    assert w_in.shape[2] == 8 * MIX + 2 * D
    tm = min(512, S)
    assert S % tm == 0 and tm % CHUNK == 0

    mod, lbv, lam = _prep(c, w_ada[0], b_ada, lb_logits, lambda_q1, lambda_k1, lambda_q2, lambda_k2)
    mod3 = mod.reshape(B, 3, D)

    o_a, dqt, dk, dvt, sdg, gma, gmb = _inproj(
        x, mod3, g_pre, lbv, w_in[0].astype(BF16), hg_norm_gain, tm, tm // 2)
    o_b_t = _attn(lam[0, :1], dqt, dk, dvt)

    return _outproj(o_a, o_b_t, sdg, diff_norm_gain, gma, gmb, x, mod3, g_post,
                    w_branch_a[0].astype(BF16), w_branch_b[0].astype(BF16), w_out[0].astype(BF16),
                    min(2 * tm, S))
```

```python
import functools
import math

import numpy as np
import jax
import jax.numpy as jnp
from jax import lax
from jax.experimental import pallas as pl
from jax.experimental.pallas import tpu as pltpu

F32 = jnp.float32
BF16 = jnp.bfloat16
RMS_EPS = 1e-6
LOG2E = 1.4426950408889634

HEAD = 128
DIFF_HEAD_DIM = 64
N_HEADS = 4
MIX = N_HEADS * HEAD
CHUNK = 128
PROJ_SLICE = 512
LEVELS = (2, 4, 8, 16, 32, 64)
CODE_SUB = 1 + len(LEVELS)
CODE_DIAG = 2 + len(LEVELS)
LAM_INIT = 0.8 - 0.6 * math.exp(-0.3 * 0)
ONES_ROWS = 16
ATTN_HEADS_PER_STEP = 2
OUT_ROW_CHUNK = 256

V7X_VMEM_BYTES = 64 * 1024 * 1024
NT_DIMS = (((1,), (1,)), ((), ()))


def _sigmoid(z):
    return jax.nn.sigmoid(z)


def _prep_kernel(c_ref, w_ref, b_ref, lbl_ref, lq1_ref, lk1_ref, lq2_ref, lk2_ref,
                 mod_ref, lb_ref, lam_ref):
    c = c_ref[...]
    sc = c * _sigmoid(c)
    mod_ref[...] = jnp.dot(sc, w_ref[...], precision=lax.Precision.HIGHEST,
                           preferred_element_type=F32) + b_ref[...]
    z = lbl_ref[...]
    e = jnp.exp(z - jnp.max(z, axis=0, keepdims=True))
    lb_ref[...] = e / jnp.sum(e, axis=0, keepdims=True)
    d1 = jnp.sum(lq1_ref[...] * lk1_ref[...], axis=-1, keepdims=True)
    d2 = jnp.sum(lq2_ref[...] * lk2_ref[...], axis=-1, keepdims=True)
    lam = jnp.exp(d1) - jnp.exp(d2) + LAM_INIT
    lam_ref[...] = jnp.broadcast_to(lam, lam_ref.shape)


def _prep(c, w_ada, b_ada, lb_logits, lq1, lk1, lq2, lk2):
    B, D = c.shape
    n_tiles = w_ada.shape[1] // D
    const = lambda j: (0, 0)
    return pl.pallas_call(
        _prep_kernel,
        grid=(n_tiles,),
        in_specs=[
            pl.BlockSpec((B, D), const),
            pl.BlockSpec((D, D), lambda j: (0, j)),
            pl.BlockSpec((1, D), lambda j: (0, j)),
            pl.BlockSpec(lb_logits.shape, const),
            pl.BlockSpec(lq1.shape, const),
            pl.BlockSpec(lk1.shape, const),
            pl.BlockSpec(lq2.shape, const),
            pl.BlockSpec(lk2.shape, const),
        ],
        out_specs=[
            pl.BlockSpec((B, D), lambda j: (0, j)),
            pl.BlockSpec(lb_logits.shape, const),
            pl.BlockSpec((1, HEAD), const),
        ],
        out_shape=[
            jax.ShapeDtypeStruct((B, n_tiles * D), F32),
            jax.ShapeDtypeStruct(lb_logits.shape, F32),
            jax.ShapeDtypeStruct((1, HEAD), F32),
        ],
        compiler_params=pltpu.CompilerParams(dimension_semantics=("arbitrary",)),
        name="prep",
    )(c, w_ada, b_ada, lb_logits, lq1, lk1, lq2, lk2)


def _inproj_kernel(x_ref, mod_ref, gpre_ref, lb_ref, w_ref, gain_ref, tril_ref, lvl_ref,
                   oa_ref, dqt_ref, dk_ref, dvt_ref, sdg_ref, gma_ref, gmb_ref,
                   hq_ref, kin_ref, logf_ref, hi_ref, sg_ref, state_ref):
    @pl.when(pl.program_id(1) == 0)
    def _():
        state_ref[...] = jnp.zeros_like(state_ref)

    x = x_ref[...]
    ms = jnp.mean(x * x, axis=-1, keepdims=True)
    xn = x * lax.rsqrt(ms + RMS_EPS) * gpre_ref[...]
    h = xn * (1.0 + mod_ref[1:2, :]) + mod_ref[0:1, :]
    hb = h.astype(BF16)

    def proj(c0, width):
        return jnp.dot(hb, w_ref[:, c0:c0 + width], preferred_element_type=F32)

    hq_ref[...] = proj(0 * MIX, MIX).astype(BF16)
    sig = _sigmoid(proj(1 * MIX, MIX))
    lb = lb_ref[0:1, :]
    oml = lb_ref[1:2, :]
    logf_ref[...] = jnp.log(lb + oml * sig)
    kin_ref[...] = (oml * (1.0 - sig)).astype(BF16)
    hi_ref[...] = proj(2 * MIX, MIX).astype(BF16)
    z = proj(3 * MIX, MIX)
    sg_ref[...] = (z * _sigmoid(z)).astype(BF16)
    tk = dvt_ref.shape[-1]
    d_model = gma_ref.shape[-1]

    def q_slice(c, z):
        dqt_ref[c:c + PROJ_SLICE, :] = (z * (DIFF_HEAD_DIM ** -0.5 * LOG2E)).T.astype(BF16)

    def k_slice(c, z):
        dk_ref[:, c:c + PROJ_SLICE] = z.astype(BF16)

    def v_slice(c, z):
        z_t = z.T.astype(BF16)
        for u in range(dvt_ref.shape[0]):
            dvt_ref[u, c:c + PROJ_SLICE, :] = z_t[:, u * tk:(u + 1) * tk]

    def g_slice(c, z):
        sdg_ref[:, c:c + PROJ_SLICE] = (z * _sigmoid(z)).astype(BF16)

    def ma_slice(c, z):
        gma_ref[:, c:c + PROJ_SLICE] = _sigmoid(z).astype(BF16)

    def mb_slice(c, z):
        gmb_ref[:, c:c + PROJ_SLICE] = _sigmoid(z).astype(BF16)

    groups = ((q_slice, MIX), (k_slice, MIX), (v_slice, MIX), (g_slice, MIX),
              (ma_slice, d_model), (mb_slice, d_model))
    pieces, col0 = [], 4 * MIX
    for fn, width in groups:
        for c in range(0, width, PROJ_SLICE):
            pieces.append((functools.partial(proj, col0 + c, PROJ_SLICE), functools.partial(fn, c)))
        col0 += width
    _hgrn2_tile(hq_ref, kin_ref, logf_ref, hi_ref, sg_ref, gain_ref, tril_ref, lvl_ref,
                oa_ref, state_ref, pieces)


def _inproj(x, mod3, g_pre, lbv, w_in_bf, gain_a, tm, tk):
    B, S, D = x.shape
    n_in = w_in_bf.shape[1]
    nt = S // tm
    ku = tm // tk
    tril = jnp.asarray(np.tril(np.ones((CHUNK, CHUNK), np.float32)), BF16)
    lvl = jnp.asarray(_level_codes())
    tok = lambda width: pl.BlockSpec((None, tm, width), lambda b, i: (b, i, 0))
    const2 = lambda b, i: (0, 0)
    sds = lambda width, dt: jax.ShapeDtypeStruct((B, S, width), dt)
    return pl.pallas_call(
        _inproj_kernel,
        grid=(B, nt),
        in_specs=[
            tok(D),
            pl.BlockSpec((None, 3, D), lambda b, i: (b, 0, 0)),
            pl.BlockSpec((1, D), const2),
            pl.BlockSpec(lbv.shape, const2),
            pl.BlockSpec((D, n_in), const2, pipeline_mode=pl.Buffered(1)),
            pl.BlockSpec((1, MIX), const2),
            pl.BlockSpec((CHUNK, CHUNK), const2),
            pl.BlockSpec((CHUNK, CHUNK), const2),
        ],
        out_specs=[
            tok(MIX),
            pl.BlockSpec((None, None, MIX, tm), lambda b, i: (b, i, 0, 0)),
            tok(MIX),
            pl.BlockSpec((None, ku, MIX, tk), lambda b, i: (b, i, 0, 0)),
            tok(MIX), tok(D), tok(D),
        ],
        out_shape=[
            sds(MIX, BF16),
            jax.ShapeDtypeStruct((B, nt, MIX, tm), BF16),
            sds(MIX, BF16),
            jax.ShapeDtypeStruct((B, nt * ku, MIX, tk), BF16),
            sds(MIX, BF16), sds(D, BF16), sds(D, BF16),
        ],
        scratch_shapes=[
            pltpu.VMEM((tm, MIX), BF16), pltpu.VMEM((tm, MIX), BF16), pltpu.VMEM((tm, MIX), F32),
            pltpu.VMEM((tm, MIX), BF16), pltpu.VMEM((tm, MIX), BF16),
            pltpu.VMEM((N_HEADS, HEAD, HEAD), F32),
        ],
        compiler_params=pltpu.CompilerParams(
            dimension_semantics=("arbitrary", "arbitrary"),
            vmem_limit_bytes=V7X_VMEM_BYTES * 7 // 8),
        name="inproj_hgrn2",
    )(x, mod3, g_pre, lbv, w_in_bf, gain_a, tril, lvl)


def _level_codes():
    t = np.arange(CHUNK)[:, None]
    s = np.arange(CHUNK)[None, :]
    code = np.zeros((CHUNK, CHUNK), np.int32)
    for i, m in reversed(list(enumerate(LEVELS))):
        code = np.where((t // (2 * m) == s // (2 * m)) & (s < t), 1 + i, code)
    code = np.where((t // 2 == s // 2) & (s == t - 1), CODE_SUB, code)
    code = np.where(s == t, CODE_DIAG, code)
    return code.astype(np.int32)


def _split3(x):
    h1 = x.astype(BF16)
    r1 = x - h1.astype(F32)
    h2 = r1.astype(BF16)
    h3 = (r1 - h2.astype(F32)).astype(BF16)
    return h1, h2, h3


def _block_row(a, block, row):
    if block < 8:
        pos = lax.broadcasted_iota(jnp.int32, a.shape, 0) % block
        out = a
        for r in range(block):
            if r != row:
                out = jnp.where(pos == r, pltpu.roll(a, (r - row) % CHUNK, axis=0), out)
        return out
    g = CHUNK // block
    a3 = a.reshape(g, block, HEAD)
    return jnp.broadcast_to(a3[:, row:row + 1, :], (g, block, HEAD)).reshape(CHUNK, HEAD)


def _hgrn2_tile(hq_ref, kin_ref, logf_ref, hi_ref, sg_ref, gain_ref, tril_ref, lvl_ref,
                out_ref, state_ref, interleaved=()):
    pending = list(interleaved)
    tril = tril_ref[...]
    lvl = lvl_ref[...]
    n_chunks = hq_ref.shape[0] // CHUNK
    instances = [(ci, h) for ci in range(n_chunks) for h in range(N_HEADS)]
    stride = max(1, len(instances) // max(1, len(pending)))

    lf_chunks, b_chunks = [], []
    for ci in range(n_chunks):
        lf_all = logf_ref[ci * CHUNK:(ci + 1) * CHUNK, :] * LOG2E
        l1, l2, l3 = _split3(lf_all)
        lf_chunks.append(lf_all)
        b_chunks.append(jnp.dot(tril, l3, preferred_element_type=F32)
                        + jnp.dot(tril, l2, preferred_element_type=F32)
                        + jnp.dot(tril, l1, preferred_element_type=F32))

    def operands(ci, h):
        rows = slice(ci * CHUNK, (ci + 1) * CHUNK)
        cols = slice(h * HEAD, (h + 1) * HEAD)
        b = b_chunks[ci][:, cols]
        qb = hq_ref[rows, cols]
        kb = kin_ref[rows, cols]
        vb = hi_ref[rows, cols]
        q = qb.astype(F32)
        k = kb.astype(F32)
        b_last = b[CHUNK - 1:CHUNK, :]
        pairs = [(qb, kb), ((q * jnp.exp2(lf_chunks[ci][:, cols])).astype(BF16), kb)]
        for m in LEVELS:
            e = jnp.exp2(b - _block_row(b, 2 * m, m - 1))
            pairs.append(((q * e).astype(BF16), (k * (1.0 / e)).astype(BF16)))
        return dict(rows=rows, cols=cols, h=h, vb=vb, pairs=pairs,
                    qbar=(q * jnp.exp2(b)).astype(BF16),
                    kbar=(k * jnp.exp2(b_last - b)).astype(BF16),
                    v_t=vb.astype(F32).T.astype(BF16),
                    state_decay=jnp.exp2(b_last))

    def score_dots(op):
        state_t = state_ref[op["h"]]
        op["o_inter"] = lax.dot_general(op["qbar"], state_t.astype(BF16), NT_DIMS,
                                        preferred_element_type=F32)
        state_ref[op["h"]] = state_t * op["state_decay"] + jnp.dot(op["v_t"], op["kbar"],
                                                                   preferred_element_type=F32)
        op["parts"] = [lax.dot_general(qm, km, NT_DIMS, preferred_element_type=F32)
                       for qm, km in op["pairs"]]

    def finish(op):
        codes = [CODE_DIAG, CODE_SUB] + [1 + i for i in range(len(LEVELS))]
        scores = jnp.where(lvl == codes[0], op["parts"][0], 0.0)
        for code, part in zip(codes[1:], op["parts"][1:]):
            scores = jnp.where(lvl == code, part, scores)
        o = op["o_inter"] + jnp.dot(scores.astype(BF16), op["vb"], preferred_element_type=F32)
        o = o * lax.rsqrt(jnp.mean(o * o, axis=-1, keepdims=True) + RMS_EPS) * gain_ref[:, op["cols"]]
        out_ref[op["rows"], op["cols"]] = (o * sg_ref[op["rows"], op["cols"]].astype(F32)).astype(BF16)

    previous = []
    for g in range(0, len(instances), stride):
        group = [operands(ci, h) for ci, h in instances[g:g + stride]]
        piece = pending.pop(0) if pending else None
        if piece is not None:
            big = piece[0]()
        for op in group:
            score_dots(op)
        if piece is not None:
            piece[1](big)
        for op in previous:
            finish(op)
        previous = group
    for op in previous:
        finish(op)
    for dot, epilogue in pending:
        epilogue(dot())


def _attn_kernel(lam_ref, qt_ref, k_ref, vt_ref, ot_ref,
                 m_ref, acc_ref, s0_ref, s1_ref, qzt_ref):
    nq, _, tq = qt_ref.shape
    tk = vt_ref.shape[-1]
    i = pl.program_id(2)
    heads = range(ATTN_HEADS_PER_STEP)
    ones_rows = jnp.ones((ONES_ROWS, tk), BF16)

    def head_rows(hh):
        return slice(hh * HEAD, (hh + 1) * HEAD)

    def first_scores(qi, hh):
        qt = qt_ref[qi, head_rows(hh), :].astype(F32)
        row = lax.broadcasted_iota(jnp.int32, qt.shape, 0)
        qzt_ref[hh] = jnp.concatenate([jnp.where(row < DIFF_HEAD_DIM, qt, 0.0),
                                       jnp.where(row >= DIFF_HEAD_DIM, qt, 0.0)], axis=1).astype(BF16)
        s0_ref[hh] = jnp.dot(k_ref[0:tk, head_rows(hh)], qzt_ref[hh], preferred_element_type=F32)

    @pl.when(i == 0)
    def _():
        for hh in heads:
            first_scores(0, hh)

    m_ref[...] = jnp.full(m_ref.shape, -jnp.inf, F32)
    acc_ref[...] = jnp.zeros(acc_ref.shape, F32)

    def key_block(j, hh):
        return k_ref[pl.ds(pl.multiple_of(j * tk, tk), tk), head_rows(hh)]

    def scores(j, s_ref, hh):
        s_ref[hh, :, 0:2 * tq] = jnp.dot(key_block(j, hh), qzt_ref[hh], preferred_element_type=F32)

    def accumulate(j, s, hh, lanes=slice(None)):
        m_old = m_ref[hh, :, lanes]
        m_new = jnp.maximum(m_old, jnp.max(s, axis=0, keepdims=True))
        alpha = jnp.exp2(m_old - m_new)
        p = jnp.exp2(s - m_new).astype(BF16)
        vt = jnp.concatenate([vt_ref[j, head_rows(hh), :], ones_rows], axis=0)
        acc_ref[hh, :, lanes] = alpha * acc_ref[hh, :, lanes] + jnp.dot(vt, p, preferred_element_type=F32)
        m_ref[hh, :, lanes] = m_new

    def pair(jj, hh):
        scores(2 * jj + 1, s1_ref, hh)
        accumulate(2 * jj, s0_ref[hh], hh)
        scores(2 * jj + 2, s0_ref, hh)
        accumulate(2 * jj + 1, s1_ref[hh, :, 0:2 * tq], hh)

    def four_pairs(t, carry):
        for hh in heads:
            for u in range(4):
                pair(4 * t + u, hh)
        return carry

    def two_pairs(t, carry):
        for hh in heads:
            pair(i // 4 * 4, hh)
            pair(i // 4 * 4 + 1, hh)
        return carry

    def one_pair(t, carry):
        for hh in heads:
            pair(i - 1, hh)
        return carry

    lax.fori_loop(0, i // 4, four_pairs, 0)
    lax.fori_loop(0, i % 4 // 2, two_pairs, 0)
    lax.fori_loop(0, i % 2, one_pair, 0)

    upper = [slice(mi * tq + tk, (mi + 1) * tq) for mi in range(2)]
    for hh in heads:
        kb_last = key_block(2 * i + 1, hh)
        for lanes in upper:
            s1_ref[hh, :, lanes] = jnp.dot(kb_last, qzt_ref[hh, :, lanes], preferred_element_type=F32)

    causal = (lax.broadcasted_iota(jnp.int32, (tk, tk), 0)
              <= lax.broadcasted_iota(jnp.int32, (tk, tk), 1))
    for hh in heads:
        s = s0_ref[hh]
        s = jnp.concatenate([jnp.where(causal, s[:, 0:tk], -jnp.inf), s[:, tk:tq],
                             jnp.where(causal, s[:, tq:tq + tk], -jnp.inf), s[:, tq + tk:]], axis=1)
        accumulate(2 * i, s, hh)
        first_scores(jnp.minimum(i + 1, nq - 1), hh)

    for hh in heads:
        for lanes in upper:
            accumulate(2 * i + 1, jnp.where(causal, s1_ref[hh, :, lanes], -jnp.inf), hh, lanes)

    for hh in heads:
        acc = acc_ref[hh]
        o_t = acc[0:HEAD, :] * (1.0 / acc[HEAD:HEAD + 1, :])
        o = o_t[:, :tq] - lam_ref[0] * o_t[:, tq:]
        ot_ref[head_rows(hh), :] = (o * lax.rsqrt(jnp.mean(o * o, axis=0, keepdims=True) + RMS_EPS)
                                    ).astype(BF16)


def _attn(lam, dqt, dk, dvt):
    B, S, _ = dk.shape
    nq, tq = dqt.shape[1], dqt.shape[3]
    nkb, tk = dvt.shape[1], dvt.shape[3]
    assert tq == 2 * tk, "the diagonal of a query block is covered by exactly two key sub-blocks"
    hps = ATTN_HEADS_PER_STEP
    width = hps * HEAD
    return pl.pallas_call(
        _attn_kernel,
        grid=(B, N_HEADS // hps, nq),
        in_specs=[
            pl.BlockSpec(memory_space=pltpu.SMEM),
            pl.BlockSpec((None, nq, width, tq), lambda b, g, i: (b, 0, g, 0)),
            pl.BlockSpec((None, S, width), lambda b, g, i: (b, 0, g)),
            pl.BlockSpec((None, nkb, width, tk), lambda b, g, i: (b, 0, g, 0)),
        ],
        out_specs=pl.BlockSpec((None, width, tq), lambda b, g, i: (b, g, i)),
        out_shape=jax.ShapeDtypeStruct((B, MIX, S), BF16),
        scratch_shapes=[pltpu.VMEM((hps, 1, 2 * tq), F32),
                        pltpu.VMEM((hps, HEAD + ONES_ROWS, 2 * tq), F32),
                        pltpu.VMEM((hps, tk, 2 * tq), F32),
                        pltpu.VMEM((hps, tk, 2 * tq + HEAD), F32),
                        pltpu.VMEM((hps, HEAD, 2 * tq), BF16)],
        compiler_params=pltpu.CompilerParams(
            dimension_semantics=("arbitrary", "arbitrary", "arbitrary"),
            vmem_limit_bytes=V7X_VMEM_BYTES * 3 // 4),
        name="attn",
    )(lam, dqt, dk, dvt)


def _outproj_kernel(oa_ref, obt_ref, sdg_ref, gb_ref, gma_ref, gmb_ref, x_ref, mod_ref, gpost_ref,
                    wa_ref, wb_ref, wo_ref, out_ref):
    n_chunks = out_ref.shape[0] // OUT_ROW_CHUNK

    def matmuls(r):
        rows = slice(r * OUT_ROW_CHUNK, (r + 1) * OUT_ROW_CHUNK)
        ya = jnp.dot(oa_ref[rows, :], wa_ref[...], preferred_element_type=F32)
        ob = (obt_ref[:, rows].astype(F32).T * (gb_ref[...] * (1.0 - LAM_INIT))
              * sdg_ref[rows, :].astype(F32))
        yb = jnp.dot(ob.astype(BF16), wb_ref[...], preferred_element_type=F32)
        y = gma_ref[rows, :].astype(F32) * ya + gmb_ref[rows, :].astype(F32) * yb
        return jnp.dot(y.astype(BF16), wo_ref[...], preferred_element_type=F32)

    def epilogue(r, y):
        rows = slice(r * OUT_ROW_CHUNK, (r + 1) * OUT_ROW_CHUNK)
        yn = y * lax.rsqrt(jnp.mean(y * y, axis=-1, keepdims=True) + RMS_EPS) * gpost_ref[...]
        out_ref[rows, :] = x_ref[rows, :] + mod_ref[2:3, :] * yn

    previous = None
    for r in range(n_chunks):
        y = matmuls(r)
        if previous is not None:
            epilogue(*previous)
        previous = (r, y)
    epilogue(*previous)


def _outproj(oa, obt, sdg, gain_b, gma, gmb, x, mod3, g_post, wa, wb, wo, tm):
    B, S, D = x.shape
    assert S % tm == 0 and tm % OUT_ROW_CHUNK == 0
    tok = lambda width: pl.BlockSpec((None, tm, width), lambda b, i: (b, i, 0))
    const2 = lambda b, i: (0, 0)
    return pl.pallas_call(
        _outproj_kernel,
        grid=(B, S // tm),
        in_specs=[
            tok(MIX),
            pl.BlockSpec((None, MIX, tm), lambda b, i: (b, 0, i)),
            tok(MIX),
            pl.BlockSpec((1, MIX), const2),
            tok(D), tok(D), tok(D),
            pl.BlockSpec((None, 3, D), lambda b, i: (b, 0, 0)),
            pl.BlockSpec((1, D), const2),
            pl.BlockSpec((MIX, D), const2),
            pl.BlockSpec((MIX, D), const2),
            pl.BlockSpec((D, D), const2),
        ],
        out_specs=tok(D),
        out_shape=jax.ShapeDtypeStruct((B, S, D), x.dtype),
        compiler_params=pltpu.CompilerParams(
            dimension_semantics=("parallel", "parallel"),
            vmem_limit_bytes=V7X_VMEM_BYTES * 3 // 4),
        name="outproj",
    )(oa, obt, sdg, gain_b, gma, gmb, x, mod3, g_post, wa, wb, wo)


def kernel(x, c, w_ada, b_ada, g_pre, g_post, w_in, lb_logits, hg_norm_gain,
           lambda_q1, lambda_k1, lambda_q2, lambda_k2, diff_norm_gain,
           w_branch_a, w_branch_b, w_out):
    B, S, D = x.shape
    assert w_in.shape[0] == 1 and lb_logits.shape[0] == 2, "single-layer block"
    assert w_in.shape[2] == 8 * MIX + 2 * D
    tm = min(512, S)
    assert S % tm == 0 and tm % CHUNK == 0

    mod, lbv, lam = _prep(c, w_ada[0], b_ada, lb_logits, lambda_q1, lambda_k1, lambda_q2, lambda_k2)
    mod3 = mod.reshape(B, 3, D)

    o_a, dqt, dk, dvt, sdg, gma, gmb = _inproj(
        x, mod3, g_pre, lbv, w_in[0].astype(BF16), hg_norm_gain, tm, tm // 2)
    o_b_t = _attn(lam[0, :1], dqt, dk, dvt)

    return _outproj(o_a, o_b_t, sdg, diff_norm_gain, gma, gmb, x, mod3, g_post,
                    w_branch_a[0].astype(BF16), w_branch_b[0].astype(BF16), w_out[0].astype(BF16),
                    min(2 * tm, S))
```

```python
import functools
import math

import numpy as np
import jax
import jax.numpy as jnp
from jax import lax
from jax.experimental import pallas as pl
from jax.experimental.pallas import tpu as pltpu

F32 = jnp.float32
BF16 = jnp.bfloat16
RMS_EPS = 1e-6
LOG2E = 1.4426950408889634

HEAD = 128
DIFF_HEAD_DIM = 64
N_HEADS = 4
MIX = N_HEADS * HEAD
CHUNK = 128
PROJ_SLICE = 512
LEVELS = (2, 4, 8, 16, 32, 64)
CODE_SUB = 1 + len(LEVELS)
CODE_DIAG = 2 + len(LEVELS)
LAM_INIT = 0.8 - 0.6 * math.exp(-0.3 * 0)
ONES_ROWS = 16
ATTN_HEADS_PER_STEP = 4
OUT_ROW_CHUNK = 256

V7X_VMEM_BYTES = 64 * 1024 * 1024
NT_DIMS = (((1,), (1,)), ((), ()))


def _sigmoid(z):
    return jax.nn.sigmoid(z)


def _prep_kernel(c_ref, w_ref, b_ref, lbl_ref, lq1_ref, lk1_ref, lq2_ref, lk2_ref,
                 mod_ref, lb_ref, lam_ref):
    c = c_ref[...]
    sc = c * _sigmoid(c)
    mod_ref[...] = jnp.dot(sc, w_ref[...], precision=lax.Precision.HIGHEST,
                           preferred_element_type=F32) + b_ref[...]
    z = lbl_ref[...]
    e = jnp.exp(z - jnp.max(z, axis=0, keepdims=True))
    lb_ref[...] = e / jnp.sum(e, axis=0, keepdims=True)
    d1 = jnp.sum(lq1_ref[...] * lk1_ref[...], axis=-1, keepdims=True)
    d2 = jnp.sum(lq2_ref[...] * lk2_ref[...], axis=-1, keepdims=True)
    lam = jnp.exp(d1) - jnp.exp(d2) + LAM_INIT
    lam_ref[...] = jnp.broadcast_to(lam, lam_ref.shape)


def _prep(c, w_ada, b_ada, lb_logits, lq1, lk1, lq2, lk2):
    B, D = c.shape
    n_tiles = w_ada.shape[1] // D
    const = lambda j: (0, 0)
    return pl.pallas_call(
        _prep_kernel,
        grid=(n_tiles,),
        in_specs=[
            pl.BlockSpec((B, D), const),
            pl.BlockSpec((D, D), lambda j: (0, j)),
            pl.BlockSpec((1, D), lambda j: (0, j)),
            pl.BlockSpec(lb_logits.shape, const),
            pl.BlockSpec(lq1.shape, const),
            pl.BlockSpec(lk1.shape, const),
            pl.BlockSpec(lq2.shape, const),
            pl.BlockSpec(lk2.shape, const),
        ],
        out_specs=[
            pl.BlockSpec((B, D), lambda j: (0, j)),
            pl.BlockSpec(lb_logits.shape, const),
            pl.BlockSpec((1, HEAD), const),
        ],
        out_shape=[
            jax.ShapeDtypeStruct((B, n_tiles * D), F32),
            jax.ShapeDtypeStruct(lb_logits.shape, F32),
            jax.ShapeDtypeStruct((1, HEAD), F32),
        ],
        compiler_params=pltpu.CompilerParams(dimension_semantics=("arbitrary",)),
        name="prep",
    )(c, w_ada, b_ada, lb_logits, lq1, lk1, lq2, lk2)


def _inproj_kernel(x_ref, mod_ref, gpre_ref, lb_ref, w_ref, gain_ref, tril_ref, lvl_ref,
                   oa_ref, dqt_ref, dk_ref, dvt_ref, sdg_ref, gma_ref, gmb_ref,
                   hq_ref, kin_ref, logf_ref, hi_ref, sg_ref, state_ref):
    @pl.when(pl.program_id(1) == 0)
    def _():
        state_ref[...] = jnp.zeros_like(state_ref)

    x = x_ref[...]
    ms = jnp.mean(x * x, axis=-1, keepdims=True)
    xn = x * lax.rsqrt(ms + RMS_EPS) * gpre_ref[...]
    h = xn * (1.0 + mod_ref[1:2, :]) + mod_ref[0:1, :]
    hb = h.astype(BF16)

    def proj(c0, width):
        return jnp.dot(hb, w_ref[:, c0:c0 + width], preferred_element_type=F32)

    hq_ref[...] = proj(0 * MIX, MIX).astype(BF16)
    sig = _sigmoid(proj(1 * MIX, MIX))
    lb = lb_ref[0:1, :]
    oml = lb_ref[1:2, :]
    logf_ref[...] = jnp.log(lb + oml * sig)
    kin_ref[...] = (oml * (1.0 - sig)).astype(BF16)
    hi_ref[...] = proj(2 * MIX, MIX).astype(BF16)
    z = proj(3 * MIX, MIX)
    sg_ref[...] = (z * _sigmoid(z)).astype(BF16)
    tk = dvt_ref.shape[-1]
    d_model = gma_ref.shape[-1]

    def q_slice(c, z):
        dqt_ref[c:c + PROJ_SLICE, :] = (z * (DIFF_HEAD_DIM ** -0.5 * LOG2E)).T.astype(BF16)

    def k_slice(c, z):
        dk_ref[:, c:c + PROJ_SLICE] = z.astype(BF16)

    def v_slice(c, z):
        z_t = z.T.astype(BF16)
        for u in range(dvt_ref.shape[0]):
            dvt_ref[u, c:c + PROJ_SLICE, :] = z_t[:, u * tk:(u + 1) * tk]

    def g_slice(c, z):
        sdg_ref[:, c:c + PROJ_SLICE] = (z * _sigmoid(z)).astype(BF16)

    def ma_slice(c, z):
        gma_ref[:, c:c + PROJ_SLICE] = _sigmoid(z).astype(BF16)

    def mb_slice(c, z):
        gmb_ref[:, c:c + PROJ_SLICE] = _sigmoid(z).astype(BF16)

    groups = ((q_slice, MIX), (k_slice, MIX), (v_slice, MIX), (g_slice, MIX),
              (ma_slice, d_model), (mb_slice, d_model))
    pieces, col0 = [], 4 * MIX
    for fn, width in groups:
        for c in range(0, width, PROJ_SLICE):
            pieces.append((functools.partial(proj, col0 + c, PROJ_SLICE), functools.partial(fn, c)))
        col0 += width
    _hgrn2_tile(hq_ref, kin_ref, logf_ref, hi_ref, sg_ref, gain_ref, tril_ref, lvl_ref,
                oa_ref, state_ref, pieces)


def _inproj(x, mod3, g_pre, lbv, w_in_bf, gain_a, tm, tk):
    B, S, D = x.shape
    n_in = w_in_bf.shape[1]
    nt = S // tm
    ku = tm // tk
    tril = jnp.asarray(np.tril(np.ones((CHUNK, CHUNK), np.float32)), BF16)
    lvl = jnp.asarray(_level_codes())
    tok = lambda width: pl.BlockSpec((None, tm, width), lambda b, i: (b, i, 0))
    const2 = lambda b, i: (0, 0)
    sds = lambda width, dt: jax.ShapeDtypeStruct((B, S, width), dt)
    return pl.pallas_call(
        _inproj_kernel,
        grid=(B, nt),
        in_specs=[
            tok(D),
            pl.BlockSpec((None, 3, D), lambda b, i: (b, 0, 0)),
            pl.BlockSpec((1, D), const2),
            pl.BlockSpec(lbv.shape, const2),
            pl.BlockSpec((D, n_in), const2, pipeline_mode=pl.Buffered(1)),
            pl.BlockSpec((1, MIX), const2),
            pl.BlockSpec((CHUNK, CHUNK), const2),
            pl.BlockSpec((CHUNK, CHUNK), const2),
        ],
        out_specs=[
            tok(MIX),
            pl.BlockSpec((None, None, MIX, tm), lambda b, i: (b, i, 0, 0)),
            tok(MIX),
            pl.BlockSpec((None, ku, MIX, tk), lambda b, i: (b, i, 0, 0)),
            tok(MIX), tok(D), tok(D),
        ],
        out_shape=[
            sds(MIX, BF16),
            jax.ShapeDtypeStruct((B, nt, MIX, tm), BF16),
            sds(MIX, BF16),
            jax.ShapeDtypeStruct((B, nt * ku, MIX, tk), BF16),
            sds(MIX, BF16), sds(D, BF16), sds(D, BF16),
        ],
        scratch_shapes=[
            pltpu.VMEM((tm, MIX), BF16), pltpu.VMEM((tm, MIX), BF16), pltpu.VMEM((tm, MIX), F32),
            pltpu.VMEM((tm, MIX), BF16), pltpu.VMEM((tm, MIX), BF16),
            pltpu.VMEM((N_HEADS, HEAD, HEAD), F32),
        ],
        compiler_params=pltpu.CompilerParams(
            dimension_semantics=("arbitrary", "arbitrary"),
            vmem_limit_bytes=V7X_VMEM_BYTES * 7 // 8),
        name="inproj_hgrn2",
    )(x, mod3, g_pre, lbv, w_in_bf, gain_a, tril, lvl)


def _level_codes():
    t = np.arange(CHUNK)[:, None]
    s = np.arange(CHUNK)[None, :]
    code = np.zeros((CHUNK, CHUNK), np.int32)
    for i, m in reversed(list(enumerate(LEVELS))):
        code = np.where((t // (2 * m) == s // (2 * m)) & (s < t), 1 + i, code)
    code = np.where((t // 2 == s // 2) & (s == t - 1), CODE_SUB, code)
    code = np.where(s == t, CODE_DIAG, code)
    return code.astype(np.int32)


def _split3(x):
    h1 = x.astype(BF16)
    r1 = x - h1.astype(F32)
    h2 = r1.astype(BF16)
    h3 = (r1 - h2.astype(F32)).astype(BF16)
    return h1, h2, h3


def _block_row(a, block, row):
    if block < 8:
        pos = lax.broadcasted_iota(jnp.int32, a.shape, 0) % block
        out = a
        for r in range(block):
            if r != row:
                out = jnp.where(pos == r, pltpu.roll(a, (r - row) % CHUNK, axis=0), out)
        return out
    g = CHUNK // block
    a3 = a.reshape(g, block, HEAD)
    return jnp.broadcast_to(a3[:, row:row + 1, :], (g, block, HEAD)).reshape(CHUNK, HEAD)


def _hgrn2_tile(hq_ref, kin_ref, logf_ref, hi_ref, sg_ref, gain_ref, tril_ref, lvl_ref,
                out_ref, state_ref, interleaved=()):
    pending = list(interleaved)
    tril = tril_ref[...]
    lvl = lvl_ref[...]
    n_chunks = hq_ref.shape[0] // CHUNK
    instances = [(ci, h) for ci in range(n_chunks) for h in range(N_HEADS)]
    stride = max(1, len(instances) // max(1, len(pending)))

    lf_chunks, b_chunks = [], []
    for ci in range(n_chunks):
        lf_all = logf_ref[ci * CHUNK:(ci + 1) * CHUNK, :] * LOG2E
        l1, l2, l3 = _split3(lf_all)
        lf_chunks.append(lf_all)
        b_chunks.append(jnp.dot(tril, l3, preferred_element_type=F32)
                        + jnp.dot(tril, l2, preferred_element_type=F32)
                        + jnp.dot(tril, l1, preferred_element_type=F32))

    def operands(ci, h):
        rows = slice(ci * CHUNK, (ci + 1) * CHUNK)
        cols = slice(h * HEAD, (h + 1) * HEAD)
        b = b_chunks[ci][:, cols]
        qb = hq_ref[rows, cols]
        kb = kin_ref[rows, cols]
        vb = hi_ref[rows, cols]
        q = qb.astype(F32)
        k = kb.astype(F32)
        b_last = b[CHUNK - 1:CHUNK, :]
        pairs = [(qb, kb), ((q * jnp.exp2(lf_chunks[ci][:, cols])).astype(BF16), kb)]
        for m in LEVELS:
            e = jnp.exp2(b - _block_row(b, 2 * m, m - 1))
            pairs.append(((q * e).astype(BF16), (k * (1.0 / e)).astype(BF16)))
        return dict(rows=rows, cols=cols, h=h, vb=vb, pairs=pairs,
                    qbar=(q * jnp.exp2(b)).astype(BF16),
                    kbar=(k * jnp.exp2(b_last - b)).astype(BF16),
                    v_t=vb.astype(F32).T.astype(BF16),
                    state_decay=jnp.exp2(b_last))

    def score_dots(op):
        state_t = state_ref[op["h"]]
        op["o_inter"] = lax.dot_general(op["qbar"], state_t.astype(BF16), NT_DIMS,
                                        preferred_element_type=F32)
        state_ref[op["h"]] = state_t * op["state_decay"] + jnp.dot(op["v_t"], op["kbar"],
                                                                   preferred_element_type=F32)
        op["parts"] = [lax.dot_general(qm, km, NT_DIMS, preferred_element_type=F32)
                       for qm, km in op["pairs"]]

    def finish(op):
        codes = [CODE_DIAG, CODE_SUB] + [1 + i for i in range(len(LEVELS))]
        scores = jnp.where(lvl == codes[0], op["parts"][0], 0.0)
        for code, part in zip(codes[1:], op["parts"][1:]):
            scores = jnp.where(lvl == code, part, scores)
        o = op["o_inter"] + jnp.dot(scores.astype(BF16), op["vb"], preferred_element_type=F32)
        o = o * lax.rsqrt(jnp.mean(o * o, axis=-1, keepdims=True) + RMS_EPS) * gain_ref[:, op["cols"]]
        out_ref[op["rows"], op["cols"]] = (o * sg_ref[op["rows"], op["cols"]].astype(F32)).astype(BF16)

    previous = []
    for g in range(0, len(instances), stride):
        group = [operands(ci, h) for ci, h in instances[g:g + stride]]
        piece = pending.pop(0) if pending else None
        if piece is not None:
            big = piece[0]()
        for op in group:
            score_dots(op)
        if piece is not None:
            piece[1](big)
        for op in previous:
            finish(op)
        previous = group
    for op in previous:
        finish(op)
    for dot, epilogue in pending:
        epilogue(dot())


def _attn_kernel(lam_ref, qt_ref, k_ref, vt_ref, ot_ref,
                 m_ref, acc_ref, s0_ref, s1_ref, qzt_ref):
    nq, _, tq = qt_ref.shape
    tk = vt_ref.shape[-1]
    i = pl.program_id(2)
    heads = range(ATTN_HEADS_PER_STEP)
    ones_rows = jnp.ones((ONES_ROWS, tk), BF16)

    def head_rows(hh):
        return slice(hh * HEAD, (hh + 1) * HEAD)

    def first_scores(qi, hh):
        qt = qt_ref[qi, head_rows(hh), :].astype(F32)
        row = lax.broadcasted_iota(jnp.int32, qt.shape, 0)
        qzt_ref[hh] = jnp.concatenate([jnp.where(row < DIFF_HEAD_DIM, qt, 0.0),
                                       jnp.where(row >= DIFF_HEAD_DIM, qt, 0.0)], axis=1).astype(BF16)
        s0_ref[hh] = jnp.dot(k_ref[0:tk, head_rows(hh)], qzt_ref[hh], preferred_element_type=F32)

    @pl.when(i == 0)
    def _():
        for hh in heads:
            first_scores(0, hh)

    m_ref[...] = jnp.full(m_ref.shape, -jnp.inf, F32)
    acc_ref[...] = jnp.zeros(acc_ref.shape, F32)

    def key_block(j, hh):
        return k_ref[pl.ds(pl.multiple_of(j * tk, tk), tk), head_rows(hh)]

    def scores(j, s_ref, hh):
        s_ref[hh] = jnp.dot(key_block(j, hh), qzt_ref[hh], preferred_element_type=F32)

    def accumulate(j, s, hh, lanes=slice(None)):
        m_old = m_ref[hh, :, lanes]
        m_new = jnp.maximum(m_old, jnp.max(s, axis=0, keepdims=True))
        alpha = jnp.exp2(m_old - m_new)
        p = jnp.exp2(s - m_new).astype(BF16)
        vt = jnp.concatenate([vt_ref[j, head_rows(hh), :], ones_rows], axis=0)
        acc_ref[hh, :, lanes] = alpha * acc_ref[hh, :, lanes] + jnp.dot(vt, p, preferred_element_type=F32)
        m_ref[hh, :, lanes] = m_new

    def pair(jj, hh):
        scores(2 * jj + 1, s1_ref, hh)
        accumulate(2 * jj, s0_ref[hh], hh)
        scores(2 * jj + 2, s0_ref, hh)
        accumulate(2 * jj + 1, s1_ref[hh], hh)

    def four_pairs(t, carry):
        for hh in heads:
            for u in range(4):
                pair(4 * t + u, hh)
        return carry

    def two_pairs(t, carry):
        for hh in heads:
            pair(i // 4 * 4, hh)
            pair(i // 4 * 4 + 1, hh)
        return carry

    def one_pair(t, carry):
        for hh in heads:
            pair(i - 1, hh)
        return carry

    lax.fori_loop(0, i // 4, four_pairs, 0)
    lax.fori_loop(0, i % 4 // 2, two_pairs, 0)
    lax.fori_loop(0, i % 2, one_pair, 0)

    upper = [slice(mi * tq + tk, (mi + 1) * tq) for mi in range(2)]
    for hh in heads:
        kb_last = key_block(2 * i + 1, hh)
        for lanes in upper:
            s1_ref[hh, :, lanes] = jnp.dot(kb_last, qzt_ref[hh, :, lanes], preferred_element_type=F32)

    causal = (lax.broadcasted_iota(jnp.int32, (tk, tk), 0)
              <= lax.broadcasted_iota(jnp.int32, (tk, tk), 1))
    for hh in heads:
        s = s0_ref[hh]
        s = jnp.concatenate([jnp.where(causal, s[:, 0:tk], -jnp.inf), s[:, tk:tq],
                             jnp.where(causal, s[:, tq:tq + tk], -jnp.inf), s[:, tq + tk:]], axis=1)
        accumulate(2 * i, s, hh)
        first_scores(jnp.minimum(i + 1, nq - 1), hh)

    for hh in heads:
        for lanes in upper:
            accumulate(2 * i + 1, jnp.where(causal, s1_ref[hh, :, lanes], -jnp.inf), hh, lanes)

    for hh in heads:
        acc = acc_ref[hh]
        o_t = acc[0:HEAD, :] * (1.0 / acc[HEAD:HEAD + 1, :])
        o = o_t[:, :tq] - lam_ref[0] * o_t[:, tq:]
        ot_ref[head_rows(hh), :] = (o * lax.rsqrt(jnp.mean(o * o, axis=0, keepdims=True) + RMS_EPS)
                                    ).astype(BF16)


def _attn(lam, dqt, dk, dvt):
    B, S, _ = dk.shape
    nq, tq = dqt.shape[1], dqt.shape[3]
    nkb, tk = dvt.shape[1], dvt.shape[3]
    assert tq == 2 * tk, "the diagonal of a query block is covered by exactly two key sub-blocks"
    hps = ATTN_HEADS_PER_STEP
    width = hps * HEAD
    return pl.pallas_call(
        _attn_kernel,
        grid=(B, N_HEADS // hps, nq),
        in_specs=[
            pl.BlockSpec(memory_space=pltpu.SMEM),
            pl.BlockSpec((None, nq, width, tq), lambda b, g, i: (b, 0, g, 0), pipeline_mode=pl.Buffered(1)),
            pl.BlockSpec((None, S, width), lambda b, g, i: (b, 0, g), pipeline_mode=pl.Buffered(1)),
            pl.BlockSpec((None, nkb, width, tk), lambda b, g, i: (b, 0, g, 0), pipeline_mode=pl.Buffered(1)),
        ],
        out_specs=pl.BlockSpec((None, width, tq), lambda b, g, i: (b, g, i)),
        out_shape=jax.ShapeDtypeStruct((B, MIX, S), BF16),
        scratch_shapes=[pltpu.VMEM((hps, 1, 2 * tq), F32),
                        pltpu.VMEM((hps, HEAD + ONES_ROWS, 2 * tq), F32),
                        pltpu.VMEM((hps, tk, 2 * tq), F32),
                        pltpu.VMEM((hps, tk, 2 * tq), F32),
                        pltpu.VMEM((hps, HEAD, 2 * tq), BF16)],
        compiler_params=pltpu.CompilerParams(
            dimension_semantics=("arbitrary", "arbitrary", "arbitrary"),
            vmem_limit_bytes=V7X_VMEM_BYTES * 3 // 4),
        name="attn",
    )(lam, dqt, dk, dvt)


def _outproj_kernel(oa_ref, obt_ref, sdg_ref, gb_ref, gma_ref, gmb_ref, x_ref, mod_ref, gpost_ref,
                    wa_ref, wb_ref, wo_ref, out_ref):
    n_chunks = out_ref.shape[0] // OUT_ROW_CHUNK

    def matmuls(r):
        rows = slice(r * OUT_ROW_CHUNK, (r + 1) * OUT_ROW_CHUNK)
        ya = jnp.dot(oa_ref[rows, :], wa_ref[...], preferred_element_type=F32)
        ob = (obt_ref[:, rows].astype(F32).T * (gb_ref[...] * (1.0 - LAM_INIT))
              * sdg_ref[rows, :].astype(F32))
        yb = jnp.dot(ob.astype(BF16), wb_ref[...], preferred_element_type=F32)
        y = gma_ref[rows, :].astype(F32) * ya + gmb_ref[rows, :].astype(F32) * yb
        return jnp.dot(y.astype(BF16), wo_ref[...], preferred_element_type=F32)

    def epilogue(r, y):
        rows = slice(r * OUT_ROW_CHUNK, (r + 1) * OUT_ROW_CHUNK)
        yn = y * lax.rsqrt(jnp.mean(y * y, axis=-1, keepdims=True) + RMS_EPS) * gpost_ref[...]
        out_ref[rows, :] = x_ref[rows, :] + mod_ref[2:3, :] * yn

    previous = None
    for r in range(n_chunks):
        y = matmuls(r)
        if previous is not None:
            epilogue(*previous)
        previous = (r, y)
    epilogue(*previous)


def _outproj(oa, obt, sdg, gain_b, gma, gmb, x, mod3, g_post, wa, wb, wo, tm):
    B, S, D = x.shape
    assert S % tm == 0 and tm % OUT_ROW_CHUNK == 0
    tok = lambda width: pl.BlockSpec((None, tm, width), lambda b, i: (b, i, 0))
    const2 = lambda b, i: (0, 0)
    return pl.pallas_call(
        _outproj_kernel,
        grid=(B, S // tm),
        in_specs=[
            tok(MIX),
            pl.BlockSpec((None, MIX, tm), lambda b, i: (b, 0, i)),
            tok(MIX),
            pl.BlockSpec((1, MIX), const2),
            tok(D), tok(D), tok(D),
            pl.BlockSpec((None, 3, D), lambda b, i: (b, 0, 0)),
            pl.BlockSpec((1, D), const2),
            pl.BlockSpec((MIX, D), const2),
            pl.BlockSpec((MIX, D), const2),
            pl.BlockSpec((D, D), const2),
        ],
        out_specs=tok(D),
        out_shape=jax.ShapeDtypeStruct((B, S, D), x.dtype),
        compiler_params=pltpu.CompilerParams(
            dimension_semantics=("parallel", "parallel"),
            vmem_limit_bytes=V7X_VMEM_BYTES * 3 // 4),
        name="outproj",
    )(oa, obt, sdg, gain_b, gma, gmb, x, mod3, g_post, wa, wb, wo)


def kernel(x, c, w_ada, b_ada, g_pre, g_post, w_in, lb_logits, hg_norm_gain,
           lambda_q1, lambda_k1, lambda_q2, lambda_k2, diff_norm_gain,
           w_branch_a, w_branch_b, w_out):
    B, S, D = x.shape
    assert w_in.shape[0] == 1 and lb_logits.shape[0] == 2, "single-layer block"
    assert w_in.shape[2] == 8 * MIX + 2 * D
    tm = min(512, S)
    assert S % tm == 0 and tm % CHUNK == 0

    mod, lbv, lam = _prep(c, w_ada[0], b_ada, lb_logits, lambda_q1, lambda_k1, lambda_q2, lambda_k2)
    mod3 = mod.reshape(B, 3, D)

    o_a, dqt, dk, dvt, sdg, gma, gmb = _inproj(
        x, mod3, g_pre, lbv, w_in[0].astype(BF16), hg_norm_gain, tm, tm // 2)
    o_b_t = _attn(lam[0, :1], dqt, dk, dvt)

    return _outproj(o_a, o_b_t, sdg, diff_norm_gain, gma, gmb, x, mod3, g_post,
                    w_branch_a[0].astype(BF16), w_branch_b[0].astype(BF16), w_out[0].astype(BF16),
                    min(2 * tm, S))
```
